```python
import math
import jax
import jax.numpy as jnp
from jax import lax
import numpy as np

D_MODEL = 1024
BATCH = 4
SEQ = 4096
DEPTH = 4

N_MIXERS = 3
HEAD_DIM = 64
DIFF_HEADS = D_MODEL // (2 * HEAD_DIM)
ATTN_QBLOCK = 128
DIL_HEADS = D_MODEL // HEAD_DIM
DIL_PATTERNS = ((128, 1), (512, 4), (2048, 16))
DIL_QBLOCK = 64
RWKV_HEADS = D_MODEL // HEAD_DIM
DECAY_LORA = 64
AAA_LORA = 64
GATE_LORA = 128
MEM_LEN = 256
XATT_HEADS = 4
D_FF = 2816
CONV_WIDTH = 3
ALPHA = (2 * DEPTH) ** 0.25
BETA = (8 * DEPTH) ** -0.25
LN_EPS = 1e-5
GN_EPS = 64e-5
NEG_INF = -1e30
N_A = len(range(0, DEPTH, N_MIXERS))
N_B = len(range(1, DEPTH, N_MIXERS))
N_C = len(range(2, DEPTH, N_MIXERS))

kernel_name = "hybrid_diff_dilated_rwkv7_encoder"


def layer_norm(x, g, b):
    xf = x.astype(jnp.float32)
    mu = jnp.mean(xf, -1, keepdims=True)
    var = jnp.mean(jnp.square(xf - mu), -1, keepdims=True)
    return ((xf - mu) * lax.rsqrt(var + LN_EPS) * g + b).astype(x.dtype)


def rms_norm(x, g):
    xf = x.astype(jnp.float32)
    return (xf * lax.rsqrt(jnp.mean(jnp.square(xf), -1, keepdims=True) + LN_EPS) * g).astype(x.dtype)


def alibi_slopes(n):
    return jnp.exp2(-8.0 * jnp.arange(1, n + 1, dtype=jnp.float32) / n)


def diff_attention(x, w_qkv, lam, subln, w_o, lambda_init):
    B, S, D = x.shape
    H, E = DIFF_HEADS, HEAD_DIM
    q, k, v = jnp.split(x @ w_qkv, 3, axis=-1)
    q = q.reshape(B, S, H, 2, E)
    k = k.reshape(B, S, H, 2, E)
    v = v.reshape(B, S, H, 2 * E)
    lamf = lam.astype(jnp.float32)
    lam_full = jnp.exp(jnp.sum(lamf[0] * lamf[1])) - jnp.exp(jnp.sum(lamf[2] * lamf[3])) + lambda_init
    slopes = alibi_slopes(H)
    pos = jnp.arange(S)
    nblk = S // ATTN_QBLOCK
    qb = q.reshape(B, nblk, ATTN_QBLOCK, H, 2, E).transpose(1, 0, 2, 3, 4, 5)
    starts = jnp.arange(nblk) * ATTN_QBLOCK

    def block(args):
        q_blk, start = args
        s = jnp.einsum('bqhce,bkhce->bhcqk', q_blk, k,
                       preferred_element_type=jnp.float32) * (E ** -0.5)
        dist = jnp.abs((start + jnp.arange(ATTN_QBLOCK))[:, None] - pos[None, :]).astype(jnp.float32)
        s = s - slopes[None, :, None, None, None] * dist[None, None, None]
        p = jax.nn.softmax(s, axis=-1)
        attn = p[:, :, 0] - lam_full * p[:, :, 1]
        return jnp.einsum('bhqk,bkhe->bqhe', attn.astype(v.dtype), v)

    o = lax.map(block, (qb, starts))
    o = o.transpose(1, 0, 2, 3, 4).reshape(B, S, H, 2 * E)
    o = rms_norm(o, subln) * (1.0 - lambda_init)
    return o.reshape(B, S, D) @ w_o


def dilated_group(q, k, v, window, dilation, slopes):
    B, S, H, E = q.shape
    half = window // (2 * dilation)
    L = S // dilation
    nblk = -(-L // DIL_QBLOCK)
    Lp = nblk * DIL_QBLOCK
    KB = DIL_QBLOCK + 2 * half

    def to_res(a):
        return a.reshape(B, L, dilation, H, E).transpose(0, 2, 1, 3, 4)

    qr = jnp.pad(to_res(q), ((0, 0), (0, 0), (0, Lp - L), (0, 0), (0, 0)))
    pad_k = ((0, 0), (0, 0), (half, Lp - L + half), (0, 0), (0, 0))
    kr = jnp.pad(to_res(k), pad_k)
    vr = jnp.pad(to_res(v), pad_k)
    starts = jnp.arange(nblk) * DIL_QBLOCK
    idx = starts[:, None] + jnp.arange(KB)[None, :]
    kb = kr[:, :, idx]
    vb = vr[:, :, idx]
    qb = qr.reshape(B, dilation, nblk, DIL_QBLOCK, H, E)
    s = jnp.einsum('brnqhe,brnkhe->brnhqk', qb, kb,
                   preferred_element_type=jnp.float32) * (E ** -0.5)
    u_q = starts[:, None] + jnp.arange(DIL_QBLOCK)[None, :]
    u_k = idx - half
    rel = u_k[:, None, :] - u_q[:, :, None]
    valid = (jnp.abs(rel) <= half) & (u_k[:, None, :] >= 0) & (u_k[:, None, :] < L)
    dist = (dilation * jnp.abs(rel)).astype(jnp.float32)
    bias = -slopes[None, :, None, None] * dist[:, None]
    s = jnp.where(valid[:, None], s + bias, NEG_INF)
    lse = jax.nn.logsumexp(s, axis=-1)
    p = jnp.exp(s - lse[..., None])
    o = jnp.einsum('brnhqk,brnkhe->brnqhe', p.astype(v.dtype), vb)
    o = o.reshape(B, dilation, Lp, H, E)[:, :, :L].transpose(0, 2, 1, 3, 4).reshape(B, S, H, E)
    lse = lse.transpose(0, 1, 2, 4, 3).reshape(B, dilation, Lp, H)[:, :, :L]
    lse = lse.transpose(0, 2, 1, 3).reshape(B, S, H)
    return o, lse


def dilated_attention(x, w_qkv, w_o):
    B, S, D = x.shape
    H, E, G = DIL_HEADS, HEAD_DIM, len(DIL_PATTERNS)
    qkv = (x @ w_qkv).reshape(B, S, G, 3, H, E)
    slopes = alibi_slopes(H)
    outs, lses = [], []
    for g, (window, dilation) in enumerate(DIL_PATTERNS):
        o, l = dilated_group(qkv[:, :, g, 0], qkv[:, :, g, 1], qkv[:, :, g, 2], window, dilation, slopes)
        outs.append(o)
        lses.append(l)
    wts = jax.nn.softmax(jnp.stack(lses), axis=0)
    o = jnp.sum(wts[..., None] * jnp.stack(outs).astype(jnp.float32), axis=0)
    return o.astype(x.dtype).reshape(B, S, D) @ w_o


def token_shift(x, reverse):
    if reverse:
        return jnp.pad(x[:, 1:], ((0, 0), (0, 1), (0, 0)))
    return jnp.pad(x[:, :-1], ((0, 0), (1, 0), (0, 0)))


def wkv7_scan(r, w, k, v, kk, a, reverse):
    B, S, H, N = r.shape
    seq = tuple(t.transpose(1, 0, 2, 3) for t in (r, w, k, v, kk, a))

    def step(state, inp):
        r_t, w_t, k_t, v_t, kk_t, a_t = inp
        sa = jnp.einsum('bhij,bhj->bhi', state, -kk_t)
        state = (state * w_t[:, :, None, :] + sa[..., None] * (kk_t * a_t)[:, :, None, :]
                 + v_t[..., None] * k_t[:, :, None, :])
        return state, jnp.einsum('bhij,bhj->bhi', state, r_t)

    _, y = lax.scan(step, jnp.zeros((B, H, N, N), jnp.float32), seq, reverse=reverse)
    return y.transpose(1, 0, 2, 3)


def head_group_norm(y, g, b):
    H, N = y.shape[-2], y.shape[-1]
    mu = jnp.mean(y, -1, keepdims=True)
    var = jnp.mean(jnp.square(y - mu), -1, keepdims=True)
    return (y - mu) * lax.rsqrt(var + GN_EPS) * g.reshape(H, N) + b.reshape(H, N)


def rwkv7_direction(x, mu, w_r, w_k, w_v, w0, w1, w2, a0, a1, a2, g1, g2, k_k, k_a, r_k,
                    gn_g, gn_b, reverse):
    B, S, D = x.shape
    H, N = RWKV_HEADS, HEAD_DIM
    xx = token_shift(x, reverse) - x
    xr, xw, xk, xv, xa, xg = (x + xx * mu[j] for j in range(6))
    r = xr @ w_r
    k = xk @ w_k
    v = xv @ w_v
    w_log = -jax.nn.softplus(-(w0 + jnp.tanh(xw @ w1) @ w2).astype(jnp.float32)) - 0.5
    w = jnp.exp(-jnp.exp(w_log))
    a = jax.nn.sigmoid((a0 + (xa @ a1) @ a2).astype(jnp.float32))
    g = jax.nn.sigmoid(xg @ g1) @ g2

    def heads(t):
        return t.astype(jnp.float32).reshape(B, S, H, N)

    r, k, v, w, a = heads(r), heads(k), heads(v), heads(w), heads(a)
    kk = k * k_k.astype(jnp.float32).reshape(H, N)
    kk = kk / jnp.maximum(jnp.linalg.norm(kk, axis=-1, keepdims=True), 1e-12)
    k = k * (1.0 + (a - 1.0) * k_a.astype(jnp.float32).reshape(H, N))
    y = wkv7_scan(r, w, k, v, kk, a, reverse)
    y = head_group_norm(y, gn_g, gn_b) + jnp.sum(r * k * r_k.astype(jnp.float32), -1, keepdims=True) * v
    return (y.reshape(B, S, D) * g).astype(x.dtype)


def rwkv7_time_mix(x, mu, w_rkv, w0, w1, w2, a0, a1, a2, g1, g2, k_k, k_a, r_k, gn_g, gn_b, w_o):
    fwd = rwkv7_direction(x, mu[0], w_rkv[0], w_rkv[1], w_rkv[2], w0[0], w1[0], w2[0], a0[0], a1[0], a2[0],
                          g1[0], g2[0], k_k[0], k_a[0], r_k, gn_g, gn_b, False)
    bwd = rwkv7_direction(x, mu[1], w_rkv[0], w_rkv[1], w_rkv[2], w0[1], w1[1], w2[1], a0[1], a1[1], a2[1],
                          g1[1], g2[1], k_k[1], k_a[1], r_k, gn_g, gn_b, True)
    return (fwd + bwd) @ w_o


def memory_cross_attention(x, mem, w_q, w_kv, w_o):
    B, S, D = x.shape
    M = mem.shape[1]
    H = XATT_HEADS
    E = D // H
    q = (x @ w_q).reshape(B, S, H, E)
    kv = (mem @ w_kv).reshape(B, M, 2, H, E)
    s = jnp.einsum('bshe,bmhe->bhsm', q, kv[:, :, 0], preferred_element_type=jnp.float32) * (E ** -0.5)
    p = jax.nn.softmax(s, axis=-1).astype(x.dtype)
    o = jnp.einsum('bhsm,bmhe->bshe', p, kv[:, :, 1])
    return o.reshape(B, S, D) @ w_o


def conv_glu(x, w_in, conv_w, conv_b, w_out):
    gate, val = jnp.split(x @ w_in, 2, axis=-1)
    gate = lax.conv_general_dilated(gate, conv_w[:, None, :], window_strides=(1,),
                                    padding=((CONV_WIDTH // 2, CONV_WIDTH // 2),),
                                    dimension_numbers=('NWC', 'WIO', 'NWC'),
                                    feature_group_count=gate.shape[-1]) + conv_b
    return (jax.nn.gelu(gate, approximate=False) * val) @ w_out


def setup_inputs(seed: int = 0) -> dict:
    key = jax.random.key(seed)
    keys = iter(jax.random.split(key, 48))
    D, F, H, N = D_MODEL, D_FF, RWKV_HEADS, HEAD_DIM
    G = len(DIL_PATTERNS)
    inv = D ** -0.5

    def normal(shape, scale):
        return jax.random.normal(next(keys), shape, jnp.float32) * scale

    def uniform(shape, lo, hi):
        return jax.random.uniform(next(keys), shape, jnp.float32, lo, hi)

    return {
        "x": normal((BATCH, SEQ, D), 1.0),
        "mem": normal((BATCH, MEM_LEN, D), 1.0),
        "diff_w_qkv": normal((N_A, D, 3 * D), inv),
        "diff_lambda": normal((N_A, 4, HEAD_DIM), 0.1),
        "diff_subln": 1.0 + normal((N_A, 2 * HEAD_DIM), 0.02),
        "diff_w_o": normal((N_A, D, D), inv * BETA),
        "dil_w_qkv": normal((N_B, D, G * 3 * D), inv),
        "dil_w_o": normal((N_B, D, D), inv * BETA),
        "rwkv_mu": uniform((N_C, 2, 6, D), 0.0, 1.0),
        "rwkv_w_rkv": normal((N_C, 3, D, D), inv),
        "rwkv_w0": uniform((N_C, 2, D), -5.0, 0.0),
        "rwkv_w1": normal((N_C, 2, D, DECAY_LORA), inv),
        "rwkv_w2": normal((N_C, 2, DECAY_LORA, D), 0.1 * DECAY_LORA ** -0.5),
        "rwkv_a0": normal((N_C, 2, D), 0.1),
        "rwkv_a1": normal((N_C, 2, D, AAA_LORA), inv),
        "rwkv_a2": normal((N_C, 2, AAA_LORA, D), 0.1 * AAA_LORA ** -0.5),
        "rwkv_g1": normal((N_C, 2, D, GATE_LORA), inv),
        "rwkv_g2": normal((N_C, 2, GATE_LORA, D), GATE_LORA ** -0.5),
        "rwkv_k_k": 0.85 + normal((N_C, 2, D), 0.02),
        "rwkv_k_a": 1.0 + normal((N_C, 2, D), 0.02),
        "rwkv_r_k": normal((N_C, H, N), 0.1),
        "rwkv_gn_g": 1.0 + normal((N_C, D), 0.02),
        "rwkv_gn_b": normal((N_C, D), 0.02),
        "rwkv_w_o": normal((N_C, D, D), inv * BETA),
        "xatt_w_q": normal((DEPTH, D, D), inv),
        "xatt_w_kv": normal((DEPTH, D, 2 * D), inv),
        "xatt_w_o": normal((DEPTH, D, D), inv * BETA),
        "ffn_w_in": normal((DEPTH, D, 2 * F), inv),
        "ffn_conv_w": normal((DEPTH, CONV_WIDTH, F), CONV_WIDTH ** -0.5),
        "ffn_conv_b": normal((DEPTH, F), 0.02),
        "ffn_w_out": normal((DEPTH, F, D), F ** -0.5 * BETA),
        "ln_g": 1.0 + normal((DEPTH, 3, D), 0.02),
        "ln_b": normal((DEPTH, 3, D), 0.02),
    }


def reference(x, mem, diff_w_qkv, diff_lambda, diff_subln, diff_w_o, dil_w_qkv, dil_w_o,
              rwkv_mu, rwkv_w_rkv, rwkv_w0, rwkv_w1, rwkv_w2, rwkv_a0, rwkv_a1, rwkv_a2,
              rwkv_g1, rwkv_g2, rwkv_k_k, rwkv_k_a, rwkv_r_k, rwkv_gn_g, rwkv_gn_b, rwkv_w_o,
              xatt_w_q, xatt_w_kv, xatt_w_o, ffn_w_in, ffn_conv_w, ffn_conv_b, ffn_w_out,
              ln_g, ln_b):
    for i in range(DEPTH):
        m, j = i % N_MIXERS, i // N_MIXERS
        if m == 0:
            lambda_init = 0.8 - 0.6 * math.exp(-0.3 * i)
            h = diff_attention(x, diff_w_qkv[j], diff_lambda[j], diff_subln[j], diff_w_o[j], lambda_init)
        elif m == 1:
            h = dilated_attention(x, dil_w_qkv[j], dil_w_o[j])
        else:
            h = rwkv7_time_mix(x, rwkv_mu[j], rwkv_w_rkv[j], rwkv_w0[j], rwkv_w1[j], rwkv_w2[j],
                               rwkv_a0[j], rwkv_a1[j], rwkv_a2[j], rwkv_g1[j], rwkv_g2[j],
                               rwkv_k_k[j], rwkv_k_a[j], rwkv_r_k[j], rwkv_gn_g[j], rwkv_gn_b[j],
                               rwkv_w_o[j])
        x = layer_norm(ALPHA * x + h, ln_g[i, 0], ln_b[i, 0])
        x = layer_norm(ALPHA * x + memory_cross_attention(x, mem, xatt_w_q[i], xatt_w_kv[i], xatt_w_o[i]),
                       ln_g[i, 1], ln_b[i, 1])
        x = layer_norm(ALPHA * x + conv_glu(x, ffn_w_in[i], ffn_conv_w[i], ffn_conv_b[i], ffn_w_out[i]),
                       ln_g[i, 2], ln_b[i, 2])
    return x
```

```python
import functools
import math

import jax
import jax.numpy as jnp
from jax import lax
from jax.experimental import pallas as pl
from jax.experimental.pallas import tpu as pltpu

BF16 = jnp.bfloat16
F32 = jnp.float32

HEAD_DIM = 64
N_MIXERS = 3
DIL_PATTERNS = ((128, 1), (512, 4), (2048, 16))
XATT_HEADS = 4
LN_EPS = 1e-5
GN_EPS = 64e-5
NEG_INF = -1e30
WKV_CHUNK = 64
VMEM_LIMIT_BYTES = 56 * 1024 * 1024
HIGHEST = lax.Precision.HIGHEST

NT_DIMS = (((1,), (1,)), ((), ()))
TN_DIMS = (((0,), (0,)), ((), ()))


def _params(*semantics):
    return pltpu.CompilerParams(dimension_semantics=semantics, vmem_limit_bytes=VMEM_LIMIT_BYTES)


def _dot(a, b):
    return jnp.dot(a, b, preferred_element_type=F32)


def _dot_nt(a, b, precision=None):
    return lax.dot_general(a, b, NT_DIMS, preferred_element_type=F32, precision=precision)


def _layer_norm(y, g, b):
    mu = jnp.mean(y, -1, keepdims=True)
    yc = y - mu
    var = jnp.mean(yc * yc, -1, keepdims=True)
    return yc * lax.rsqrt(var + LN_EPS) * g + b


def _mm_kernel(x_ref, w_ref, o_ref):
    o_ref[...] = _dot(x_ref[...].astype(BF16), w_ref[...]).astype(o_ref.dtype)


def matmul(x, w, out_dtype, tm, tn):
    M, K = x.shape
    N = w.shape[1]
    return pl.pallas_call(
        _mm_kernel,
        grid=(M // tm, N // tn),
        in_specs=[pl.BlockSpec((tm, K), lambda i, j: (i, 0)),
                  pl.BlockSpec((K, tn), lambda i, j: (0, j))],
        out_specs=pl.BlockSpec((tm, tn), lambda i, j: (i, j)),
        out_shape=jax.ShapeDtypeStruct((M, N), out_dtype),
        compiler_params=_params("parallel", "arbitrary"),
        name="matmul",
    )(x, w)


def _mm_res_ln_kernel(*refs, n_h, alpha):
    h_refs = refs[:n_h]
    w_ref, x_ref, g_ref, b_ref, o_ref = refs[n_h:]
    h = h_refs[0][...]
    for r in h_refs[1:]:
        h = h + r[...]
    y = alpha * x_ref[...] + _dot(h.astype(BF16), w_ref[...])
    o_ref[...] = _layer_norm(y, g_ref[...], b_ref[...])


def matmul_residual_ln(hs, w, x, g, b, alpha, tm):
    M, K = hs[0].shape
    D = w.shape[1]
    row = lambda i: (i, 0)
    fixed = lambda i: (0, 0)
    return pl.pallas_call(
        functools.partial(_mm_res_ln_kernel, n_h=len(hs), alpha=alpha),
        grid=(M // tm,),
        in_specs=[pl.BlockSpec((tm, K), row) for _ in hs]
        + [pl.BlockSpec((K, D), fixed), pl.BlockSpec((tm, D), row),
           pl.BlockSpec((1, D), fixed), pl.BlockSpec((1, D), fixed)],
        out_specs=pl.BlockSpec((tm, D), row),
        out_shape=jax.ShapeDtypeStruct((M, D), F32),
        compiler_params=_params("parallel"),
        name="matmul_residual_ln",
    )(*hs, w, x, g.reshape(1, D), b.reshape(1, D))


def _diff_attn_kernel(slope_ref, lam_ref, q_ref, k_ref, v_ref, subln_ref, o_ref, *, tq, out_scale):
    h = pl.program_id(1)
    qi = pl.program_id(2)
    E = HEAD_DIM
    slope = slope_ref[h]
    lam = lam_ref[0]
    q = q_ref[0] * (E ** -0.5)
    k = k_ref[0]
    v = v_ref[0]
    S = k.shape[0]
    lane = lax.broadcasted_iota(jnp.int32, q.shape, 1)
    zero = jnp.zeros_like(q)
    row = (qi * tq + lax.broadcasted_iota(jnp.int32, (tq, 1), 0)).astype(F32)
    col = lax.broadcasted_iota(jnp.int32, (1, S), 1).astype(F32)
    bias = slope * jnp.abs(row - col)

    def softmax_part(qc):
        s = _dot_nt(qc, k) - bias
        m = jnp.max(s, -1, keepdims=True)
        p = jnp.exp(s - m)
        return p, 1.0 / jnp.sum(p, -1, keepdims=True)

    p1, r1 = softmax_part(jnp.where(lane < E, q, zero))
    p2, r2 = softmax_part(jnp.where(lane >= E, q, zero))
    attn = p1 * r1 - p2 * (lam * r2)
    o = _dot(attn.astype(BF16), v)
    o = o * lax.rsqrt(jnp.mean(o * o, -1, keepdims=True) + LN_EPS) * subln_ref[...] * out_scale
    o_ref[0] = o.astype(o_ref.dtype)


def diff_attention_core(qkv, slopes, lam_full, subln, lambda_init, tq):
    B, S, D3 = qkv.shape
    D = D3 // 3
    H = D // (2 * HEAD_DIM)
    W = 2 * HEAD_DIM
    smem = pl.BlockSpec(memory_space=pltpu.SMEM)
    return pl.pallas_call(
        functools.partial(_diff_attn_kernel, tq=tq, out_scale=1.0 - lambda_init),
        grid=(B, H, S // tq),
        in_specs=[smem, smem,
                  pl.BlockSpec((1, tq, W), lambda b, h, i: (b, i, h)),
                  pl.BlockSpec((1, S, W), lambda b, h, i: (b, 0, H + h)),
                  pl.BlockSpec((1, S, W), lambda b, h, i: (b, 0, 2 * H + h)),
                  pl.BlockSpec((1, W), lambda b, h, i: (0, 0))],
        out_specs=pl.BlockSpec((1, tq, W), lambda b, h, i: (b, i, h)),
        out_shape=jax.ShapeDtypeStruct((B, S, D), BF16),
        compiler_params=_params("parallel", "parallel", "arbitrary"),
        name="diff_attention",
    )(slopes, lam_full, qkv, qkv, qkv, subln.reshape(1, W))


def _dil_attn_kernel(slope_ref, q_ref, k_ref, v_ref, o_ref, lse_ref, *, tq, half, dilation):
    hp = pl.program_id(2)
    qi = pl.program_id(3)
    E = HEAD_DIM
    L = k_ref.shape[1]
    kw = tq + 2 * half
    q0 = qi * tq
    ks = pl.multiple_of(jnp.clip(q0 - half, 0, L - kw), 8)
    q = q_ref[0] * (E ** -0.5)
    k = k_ref[0, pl.ds(ks, kw), :]
    v = v_ref[0, pl.ds(ks, kw), :]
    lane = lax.broadcasted_iota(jnp.int32, q.shape, 1)
    zero = jnp.zeros_like(q)
    u_q = q0 + lax.broadcasted_iota(jnp.int32, (tq, 1), 0)
    u_k = ks + lax.broadcasted_iota(jnp.int32, (1, kw), 1)
    rel = jnp.abs(u_k - u_q)
    valid = rel <= half
    dist = (dilation * rel).astype(F32)

    def one_head(qc, slope):
        s = jnp.where(valid, _dot_nt(qc, k) - slope * dist, NEG_INF)
        m = jnp.max(s, -1, keepdims=True)
        p = jnp.exp(s - m)
        l = jnp.sum(p, -1, keepdims=True)
        o = _dot(p.astype(BF16), v) * (1.0 / l)
        return o, m + jnp.log(l)

    o0, lse0 = one_head(jnp.where(lane < E, q, zero), slope_ref[2 * hp])
    o1, lse1 = one_head(jnp.where(lane >= E, q, zero), slope_ref[2 * hp + 1])
    o_ref[0] = jnp.where(lane < E, o0, o1)
    lse_ref[0] = jnp.where(lane < E, lse0, lse1)


def dilated_group_core(qkv, slopes, g, window, dilation, tq):
    B, S, C = qkv.shape
    G = len(DIL_PATTERNS)
    D = C // (3 * G)
    W = 2 * HEAD_DIM
    HP = D // W
    L = S // dilation
    half = window // (2 * dilation)
    cpt = C // W
    base = g * 3 * HP
    view = qkv.reshape(B, L, dilation * C)
    smem = pl.BlockSpec(memory_space=pltpu.SMEM)
    out_spec = pl.BlockSpec((1, tq, W), lambda b, r, hp, i: (b, i, r * HP + hp))
    o, lse = pl.pallas_call(
        functools.partial(_dil_attn_kernel, tq=tq, half=half, dilation=dilation),
        grid=(B, dilation, HP, L // tq),
        in_specs=[smem,
                  pl.BlockSpec((1, tq, W), lambda b, r, hp, i: (b, i, r * cpt + base + hp)),
                  pl.BlockSpec((1, L, W), lambda b, r, hp, i: (b, 0, r * cpt + base + HP + hp)),
                  pl.BlockSpec((1, L, W), lambda b, r, hp, i: (b, 0, r * cpt + base + 2 * HP + hp))],
        out_specs=[out_spec, out_spec],
        out_shape=[jax.ShapeDtypeStruct((B, L, dilation * D), F32)] * 2,
        compiler_params=_params("parallel", "parallel", "parallel", "arbitrary"),
        name=f"dilated_attention_d{dilation}",
    )(slopes, view, view, view)
    return o.reshape(B * S, D), lse.reshape(B * S, D)


def _dil_merge_kernel(o0, o1, o2, l0, l1, l2, out_ref):
    a, b, c = l0[...], l1[...], l2[...]
    m = jnp.maximum(jnp.maximum(a, b), c)
    ea, eb, ec = jnp.exp(a - m), jnp.exp(b - m), jnp.exp(c - m)
    inv = 1.0 / (ea + eb + ec)
    out_ref[...] = ((ea * inv) * o0[...] + (eb * inv) * o1[...] + (ec * inv) * o2[...]).astype(out_ref.dtype)


def dilated_merge(outs, lses, tm):
    M, D = outs[0].shape
    spec = pl.BlockSpec((tm, D), lambda i: (i, 0))
    return pl.pallas_call(
        _dil_merge_kernel,
        grid=(M // tm,),
        in_specs=[spec] * 6,
        out_specs=spec,
        out_shape=jax.ShapeDtypeStruct((M, D), BF16),
        compiler_params=_params("parallel"),
        name="dilated_merge",
    )(*outs, *lses)


def _shifted_rows(x, prev_ref, next_ref, i, n_i, reverse):
    tm = x.shape[0]
    r = lax.broadcasted_iota(jnp.int32, (tm, 1), 0)
    if reverse:
        edge = jnp.where(i == n_i - 1, 0.0, next_ref[0, 0:1, :])
        return jnp.where(r == tm - 1, edge, pltpu.roll(x, tm - 1, 0))
    edge = jnp.where(i == 0, 0.0, prev_ref[0, 7:8, :])
    return jnp.where(r == 0, edge, pltpu.roll(x, 1, 0))


def _rwkv_proj_kernel(x_ref, prev_ref, next_ref, mu_ref, wr_ref, wk_ref, wv_ref, w0_ref, w1_ref, w2_ref,
                      a0_ref, a1_ref, a2_ref, g1_ref, g2_ref,
                      r_ref, k_ref, v_ref, lw_ref, a_ref, g_ref, *, reverse):
    i = pl.program_id(1)
    x = x_ref[0]
    xx = _shifted_rows(x, prev_ref, next_ref, i, pl.num_programs(1), reverse) - x
    mu = mu_ref[...]

    def mix(j):
        return (x + xx * mu[j:j + 1]).astype(BF16)

    r_ref[0] = _dot(mix(0), wr_ref[...])
    k_ref[0] = _dot(mix(2), wk_ref[...])
    v_ref[0] = _dot(mix(3), wv_ref[...])
    z = -(w0_ref[...] + _dot(jnp.tanh(_dot(mix(1), w1_ref[...])).astype(BF16), w2_ref[...]))
    softplus = jnp.maximum(z, 0.0) + jnp.log(1.0 + jnp.exp(-jnp.abs(z)))
    lw_ref[0] = -jnp.exp(-softplus - 0.5)
    a_ref[0] = jax.nn.sigmoid(a0_ref[...] + _dot(_dot(mix(4), a1_ref[...]).astype(BF16), a2_ref[...]))
    g_ref[0] = _dot(jax.nn.sigmoid(_dot(mix(5), g1_ref[...])).astype(BF16), g2_ref[...])


def rwkv_projections(x, mu, w_rkv, w0, w1, w2, a0, a1, a2, g1, g2, reverse, tm):
    B, S, D = x.shape
    nb = tm // 8
    fixed2 = lambda b, i: (0, 0)
    tile = pl.BlockSpec((1, tm, D), lambda b, i: (b, i, 0))
    full = lambda a: pl.BlockSpec(a.shape, fixed2)
    ops = (mu, w_rkv[0], w_rkv[1], w_rkv[2], w0.reshape(1, D), w1, w2, a0.reshape(1, D), a1, a2, g1, g2)
    return pl.pallas_call(
        functools.partial(_rwkv_proj_kernel, reverse=reverse),
        grid=(B, S // tm),
        in_specs=[tile,
                  pl.BlockSpec((1, 8, D), lambda b, i: (b, jnp.maximum(i * nb - 1, 0), 0)),
                  pl.BlockSpec((1, 8, D), lambda b, i: (b, jnp.minimum((i + 1) * nb, S // 8 - 1), 0))]
        + [full(a) for a in ops],
        out_specs=[tile] * 6,
        out_shape=[jax.ShapeDtypeStruct((B, S, D), F32)] * 6,
        compiler_params=_params("parallel", "arbitrary"),
        name="rwkv_projections_bwd" if reverse else "rwkv_projections_fwd",
    )(x, x, x, *ops)


def _wkv_head_chunk(state, r, k, v, cum, lw, a, kkw, kaw, rkw, gng, gnb, gate, reverse):
    C, N = r.shape
    kk = k * kkw
    kk = kk / jnp.maximum(jnp.sqrt(jnp.sum(kk * kk, -1, keepdims=True)), 1e-12)
    k = k * (1.0 + (a - 1.0) * kaw)
    dec = jnp.exp(cum)
    inv = jnp.exp(-cum)
    alpha_bar = -kk * jnp.exp(cum - lw)
    beta_t = kk * a * inv
    k_t = k * inv
    r_bar = r * dec

    ti = lax.broadcasted_iota(jnp.int32, (C, C), 0)
    si = lax.broadcasted_iota(jnp.int32, (C, C), 1)
    strict = (si > ti) if reverse else (si < ti)
    incl = (si >= ti) if reverse else (si <= ti)
    gram = _dot_nt(jnp.concatenate([alpha_bar, r_bar], 0), jnp.concatenate([beta_t, k_t], 0), HIGHEST)
    a_ab = jnp.where(strict, gram[:C, :C], 0.0)
    a_ak = jnp.where(strict, gram[:C, C:], 0.0)
    a_rb = jnp.where(incl, gram[C:, :C], 0.0)
    a_rk = jnp.where(incl, gram[C:, C:], 0.0)

    hp = functools.partial(jnp.dot, preferred_element_type=F32, precision=HIGHEST)
    eye = (ti == si).astype(F32)
    x = eye + a_ab
    p = hp(a_ab, a_ab)
    m = 2
    while m < C:
        x = x + hp(x, p)
        m *= 2
        if m < C:
            p = hp(p, p)

    u = hp(x, _dot_nt(alpha_bar, state, HIGHEST) + hp(a_ak, v))
    y = _dot_nt(r_bar, state, HIGHEST) + hp(a_rb, u) + hp(a_rk, v)
    total = cum[0:1] if reverse else cum[C - 1:C]
    end = jnp.exp(total)
    tn = functools.partial(lax.dot_general, dimension_numbers=TN_DIMS, preferred_element_type=F32,
                           precision=HIGHEST)
    new_state = state * end + tn(u, beta_t * end) + tn(v, k_t * end)

    mu = jnp.mean(y, -1, keepdims=True)
    yc = y - mu
    var = jnp.mean(yc * yc, -1, keepdims=True)
    out = yc * lax.rsqrt(var + GN_EPS) * gng + gnb
    out = out + jnp.sum(r * k * rkw, -1, keepdims=True) * v
    return new_state, out * gate


def _wkv_kernel(rf, kf, vf, lwf, af, gf, rb, kb, vb, lwb, ab, gb,
                kk_ref, ka_ref, rk_ref, gng_ref, gnb_ref, of_ref, ob_ref, state_ref):
    c = pl.program_id(2)
    N = HEAD_DIM
    C = rf.shape[1]

    @pl.when(c == 0)
    def _():
        state_ref[...] = jnp.zeros_like(state_ref)

    ti = lax.broadcasted_iota(jnp.int32, (C, C), 0)
    si = lax.broadcasted_iota(jnp.int32, (C, C), 1)
    for d, (refs, o_ref) in enumerate((((rf, kf, vf, lwf, af, gf), of_ref), ((rb, kb, vb, lwb, ab, gb), ob_ref))):
        reverse = d == 1
        r, k, v, lw, a, gate = (t[0] for t in refs)
        ones = ((si >= ti) if reverse else (si <= ti)).astype(F32)
        cum = jnp.dot(ones, lw, preferred_element_type=F32, precision=HIGHEST)
        outs = []
        for hh in range(2):
            sl = slice(hh * N, (hh + 1) * N)
            new_state, out = _wkv_head_chunk(
                state_ref[d, hh], r[:, sl], k[:, sl], v[:, sl], cum[:, sl], lw[:, sl], a[:, sl],
                kk_ref[d:d + 1, sl], ka_ref[d:d + 1, sl], rk_ref[:, sl], gng_ref[:, sl], gnb_ref[:, sl],
                gate[:, sl], reverse)
            state_ref[d, hh] = new_state
            outs.append(out)
        o_ref[0] = jnp.concatenate(outs, -1)


def rwkv_scan(fwd, bwd, k_k, k_a, r_k, gn_g, gn_b):
    B, S, D = fwd[0].shape
    W = 2 * HEAD_DIM
    C = WKV_CHUNK
    NC = S // C
    f_spec = pl.BlockSpec((1, C, W), lambda b, hp, c: (b, c, hp))
    b_spec = pl.BlockSpec((1, C, W), lambda b, hp, c: (b, NC - 1 - c, hp))
    two = pl.BlockSpec((2, W), lambda b, hp, c: (0, hp))
    one = pl.BlockSpec((1, W), lambda b, hp, c: (0, hp))
    return pl.pallas_call(
        _wkv_kernel,
        grid=(B, D // W, NC),
        in_specs=[f_spec] * 6 + [b_spec] * 6 + [two, two, one, one, one],
        out_specs=[f_spec, b_spec],
        out_shape=[jax.ShapeDtypeStruct((B, S, D), F32)] * 2,
        scratch_shapes=[pltpu.VMEM((2, 2, HEAD_DIM, HEAD_DIM), F32)],
        compiler_params=_params("parallel", "parallel", "arbitrary"),
        name="rwkv_scan",
    )(*fwd, *bwd, k_k, k_a, r_k.reshape(1, D), gn_g.reshape(1, D), gn_b.reshape(1, D))


def _xatt_kernel(x_ref, kv_ref, wq_ref, wo_ref, g_ref, b_ref, o_ref, *, alpha):
    x = x_ref[0]
    D = x.shape[-1]
    E = D // XATT_HEADS
    q = (_dot(x.astype(BF16), wq_ref[...]) * (E ** -0.5)).astype(BF16)
    outs = []
    for h in range(XATT_HEADS):
        kh = kv_ref[0, :, h * E:(h + 1) * E]
        vh = kv_ref[0, :, D + h * E:D + (h + 1) * E]
        s = _dot_nt(q[:, h * E:(h + 1) * E], kh)
        p = jnp.exp(s - jnp.max(s, -1, keepdims=True))
        inv = 1.0 / jnp.sum(p, -1, keepdims=True)
        outs.append((_dot(p.astype(BF16), vh) * inv).astype(BF16))
    y = alpha * x + _dot(jnp.concatenate(outs, -1), wo_ref[...])
    o_ref[0] = _layer_norm(y, g_ref[...], b_ref[...])


def cross_attention_sublayer(x, kv, w_q, w_o, g, b, alpha, tm):
    B, S, D = x.shape
    M = kv.shape[1]
    fixed = lambda bb, i: (0, 0)
    tile = pl.BlockSpec((1, tm, D), lambda bb, i: (bb, i, 0))
    return pl.pallas_call(
        functools.partial(_xatt_kernel, alpha=alpha),
        grid=(B, S // tm),
        in_specs=[tile, pl.BlockSpec((1, M, 2 * D), lambda bb, i: (bb, 0, 0)),
                  pl.BlockSpec((D, D), fixed), pl.BlockSpec((D, D), fixed),
                  pl.BlockSpec((1, D), fixed), pl.BlockSpec((1, D), fixed)],
        out_specs=tile,
        out_shape=jax.ShapeDtypeStruct((B, S, D), F32),
        compiler_params=_params("parallel", "arbitrary"),
        name="cross_attention",
    )(x, kv, w_q, w_o, g.reshape(1, D), b.reshape(1, D))


def _ffn_in_kernel(x_ref, prev_ref, next_ref, wg_ref, wv_ref, cw_ref, cb_ref, o_ref):
    i = pl.program_id(1)
    n_i = pl.num_programs(1)
    x = x_ref[0].astype(BF16)
    tm = x.shape[0]
    wg = wg_ref[...]
    gate = _dot(x, wg)
    val = _dot(x, wv_ref[...])
    g_prev = _dot(prev_ref[0].astype(BF16), wg)[7:8]
    g_next = _dot(next_ref[0].astype(BF16), wg)[0:1]
    g_prev = jnp.where(i == 0, 0.0, g_prev)
    g_next = jnp.where(i == n_i - 1, 0.0, g_next)
    r = lax.broadcasted_iota(jnp.int32, (tm, 1), 0)
    before = jnp.where(r == 0, g_prev, pltpu.roll(gate, 1, 0))
    after = jnp.where(r == tm - 1, g_next, pltpu.roll(gate, tm - 1, 0))
    cw = cw_ref[...]
    conv = before * cw[0:1] + gate * cw[1:2] + after * cw[2:3] + cb_ref[...]
    gelu = 0.5 * conv * (1.0 + lax.erf(conv * (2.0 ** -0.5)))
    o_ref[0] = (gelu * val).astype(o_ref.dtype)


def ffn_in(x, w_in, conv_w, conv_b, tm, tn):
    B, S, D = x.shape
    F = w_in.shape[1] // 2
    nb = tm // 8
    nj = F // tn
    return pl.pallas_call(
        _ffn_in_kernel,
        grid=(B, S // tm, nj),
        in_specs=[pl.BlockSpec((1, tm, D), lambda b, i, j: (b, i, 0)),
                  pl.BlockSpec((1, 8, D), lambda b, i, j: (b, jnp.maximum(i * nb - 1, 0), 0)),
                  pl.BlockSpec((1, 8, D), lambda b, i, j: (b, jnp.minimum((i + 1) * nb, S // 8 - 1), 0)),
                  pl.BlockSpec((D, tn), lambda b, i, j: (0, j)),
                  pl.BlockSpec((D, tn), lambda b, i, j: (0, nj + j)),
                  pl.BlockSpec((3, tn), lambda b, i, j: (0, j)),
                  pl.BlockSpec((1, tn), lambda b, i, j: (0, j))],
        out_specs=pl.BlockSpec((1, tm, tn), lambda b, i, j: (b, i, j)),
        out_shape=jax.ShapeDtypeStruct((B, S, F), BF16),
        compiler_params=_params("parallel", "parallel", "arbitrary"),
        name="ffn_in",
    )(x, x, x, w_in, w_in, conv_w, conv_b.reshape(1, F))


def _alibi_slopes(n):
    return jnp.exp2(-8.0 * jnp.arange(1, n + 1, dtype=F32) / n)


def kernel(x, mem, diff_w_qkv, diff_lambda, diff_subln, diff_w_o, dil_w_qkv, dil_w_o, rwkv_mu, rwkv_w_rkv, rwkv_w0, rwkv_w1, rwkv_w2, rwkv_a0, rwkv_a1, rwkv_a2, rwkv_g1, rwkv_g2, rwkv_k_k, rwkv_k_a, rwkv_r_k, rwkv_gn_g, rwkv_gn_b, rwkv_w_o, xatt_w_q, xatt_w_kv, xatt_w_o, ffn_w_in, ffn_conv_w, ffn_conv_b, ffn_w_out, ln_g, ln_b):
    B, S, D = x.shape
    depth = xatt_w_q.shape[0]
    alpha = (2 * depth) ** 0.25
    T = B * S
    bf = lambda a: a.astype(BF16)
    mem2 = mem.reshape(-1, D)

    for i in range(depth):
        m, j = i % N_MIXERS, i // N_MIXERS
        x2 = x.reshape(T, D)
        if m == 0:
            lambda_init = 0.8 - 0.6 * math.exp(-0.3 * i)
            lamf = diff_lambda[j].astype(F32)
            lam_full = (jnp.exp(jnp.sum(lamf[0] * lamf[1])) - jnp.exp(jnp.sum(lamf[2] * lamf[3]))
                        + lambda_init).reshape(1)
            qkv = matmul(x2, bf(diff_w_qkv[j]), BF16, tm=1024, tn=512).reshape(B, S, 3 * D)
            o = diff_attention_core(qkv, _alibi_slopes(D // (2 * HEAD_DIM)), lam_full, diff_subln[j],
                                    lambda_init, tq=256)
            hs, w_o = [o.reshape(T, D)], diff_w_o[j]
        elif m == 1:
            qkv = matmul(x2, bf(dil_w_qkv[j]), BF16, tm=1024, tn=512).reshape(B, S, -1)
            slopes = _alibi_slopes(D // HEAD_DIM)
            parts = [dilated_group_core(qkv, slopes, g, window, dilation, tq=128)
                     for g, (window, dilation) in enumerate(DIL_PATTERNS)]
            hs = [dilated_merge([p[0] for p in parts], [p[1] for p in parts], tm=512)]
            w_o = dil_w_o[j]
        else:
            dirs = [rwkv_projections(x, rwkv_mu[j, d], bf(rwkv_w_rkv[j]), rwkv_w0[j, d], bf(rwkv_w1[j, d]),
                                     bf(rwkv_w2[j, d]), rwkv_a0[j, d], bf(rwkv_a1[j, d]), bf(rwkv_a2[j, d]),
                                     bf(rwkv_g1[j, d]), bf(rwkv_g2[j, d]), reverse=d == 1, tm=256)
                    for d in range(2)]
            yf, yb = rwkv_scan(dirs[0], dirs[1], rwkv_k_k[j], rwkv_k_a[j], rwkv_r_k[j],
                               rwkv_gn_g[j], rwkv_gn_b[j])
            hs, w_o = [yf.reshape(T, D), yb.reshape(T, D)], rwkv_w_o[j]
        x2 = matmul_residual_ln(hs, bf(w_o), x2, ln_g[i, 0], ln_b[i, 0], alpha, tm=512)

        kv = matmul(mem2, bf(xatt_w_kv[i]), BF16, tm=512, tn=512).reshape(B, -1, 2 * D)
        x = cross_attention_sublayer(x2.reshape(B, S, D), kv, bf(xatt_w_q[i]), bf(xatt_w_o[i]),
                                     ln_g[i, 1], ln_b[i, 1], alpha, tm=512)

        act = ffn_in(x, bf(ffn_w_in[i]), ffn_conv_w[i], ffn_conv_b[i], tm=512, tn=1408)
        x = matmul_residual_ln([act.reshape(T, -1)], bf(ffn_w_out[i]), x.reshape(T, D),
                               ln_g[i, 2], ln_b[i, 2], alpha, tm=512).reshape(B, S, D)
    return x
```

```python
import functools
import math

import jax
import jax.numpy as jnp
from jax import lax
from jax.experimental import pallas as pl
from jax.experimental.pallas import tpu as pltpu

BF16 = jnp.bfloat16
F32 = jnp.float32

HEAD_DIM = 64
N_MIXERS = 3
DIL_PATTERNS = ((128, 1), (512, 4), (2048, 16))
XATT_HEADS = 4
LN_EPS = 1e-5
GN_EPS = 64e-5
NEG_INF = -1e30
WKV_CHUNK = 64
VMEM_LIMIT_BYTES = 56 * 1024 * 1024
HIGHEST = lax.Precision.HIGHEST

NT_DIMS = (((1,), (1,)), ((), ()))
TN_DIMS = (((0,), (0,)), ((), ()))


def _params(*semantics):
    return pltpu.CompilerParams(dimension_semantics=semantics, vmem_limit_bytes=VMEM_LIMIT_BYTES)


def _dot(a, b):
    return jnp.dot(a, b, preferred_element_type=F32)


def _dot_nt(a, b, precision=None):
    return lax.dot_general(a, b, NT_DIMS, preferred_element_type=F32, precision=precision)


def _layer_norm(y, g, b):
    mu = jnp.mean(y, -1, keepdims=True)
    yc = y - mu
    var = jnp.mean(yc * yc, -1, keepdims=True)
    return yc * lax.rsqrt(var + LN_EPS) * g + b


def _mm_kernel(x_ref, w_ref, o_ref):
    o_ref[...] = _dot(x_ref[...].astype(BF16), w_ref[...]).astype(o_ref.dtype)


def matmul(x, w, out_dtype, tm, tn):
    M, K = x.shape
    N = w.shape[1]
    return pl.pallas_call(
        _mm_kernel,
        grid=(M // tm, N // tn),
        in_specs=[pl.BlockSpec((tm, K), lambda i, j: (i, 0)),
                  pl.BlockSpec((K, tn), lambda i, j: (0, j))],
        out_specs=pl.BlockSpec((tm, tn), lambda i, j: (i, j)),
        out_shape=jax.ShapeDtypeStruct((M, N), out_dtype),
        compiler_params=_params("parallel", "arbitrary"),
        name="matmul",
    )(x, w)


def _mm_res_ln_kernel(*refs, n_h, alpha):
    h_refs = refs[:n_h]
    w_ref, x_ref, g_ref, b_ref, o_ref = refs[n_h:]
    h = h_refs[0][...]
    for r in h_refs[1:]:
        h = h + r[...]
    y = alpha * x_ref[...] + _dot(h.astype(BF16), w_ref[...])
    o_ref[...] = _layer_norm(y, g_ref[...], b_ref[...])


def matmul_residual_ln(hs, w, x, g, b, alpha, tm):
    M, K = hs[0].shape
    D = w.shape[1]
    row = lambda i: (i, 0)
    fixed = lambda i: (0, 0)
    return pl.pallas_call(
        functools.partial(_mm_res_ln_kernel, n_h=len(hs), alpha=alpha),
        grid=(M // tm,),
        in_specs=[pl.BlockSpec((tm, K), row) for _ in hs]
        + [pl.BlockSpec((K, D), fixed), pl.BlockSpec((tm, D), row),
           pl.BlockSpec((1, D), fixed), pl.BlockSpec((1, D), fixed)],
        out_specs=pl.BlockSpec((tm, D), row),
        out_shape=jax.ShapeDtypeStruct((M, D), F32),
        compiler_params=_params("parallel"),
        name="matmul_residual_ln",
    )(*hs, w, x, g.reshape(1, D), b.reshape(1, D))


def _diff_attn_kernel(slope_ref, lam_ref, q_ref, k_ref, v_ref, subln_ref, o_ref, *, tq, out_scale):
    h = pl.program_id(1)
    qi = pl.program_id(2)
    E = HEAD_DIM
    slope = slope_ref[h]
    lam = lam_ref[0]
    q = q_ref[0] * (E ** -0.5)
    k = k_ref[0]
    v = v_ref[0]
    S = k.shape[0]
    lane = lax.broadcasted_iota(jnp.int32, q.shape, 1)
    zero = jnp.zeros_like(q)
    row = (qi * tq + lax.broadcasted_iota(jnp.int32, (tq, 1), 0)).astype(F32)
    col = lax.broadcasted_iota(jnp.int32, (1, S), 1).astype(F32)
    bias = slope * jnp.abs(row - col)

    def softmax_part(qc):
        s = _dot_nt(qc, k) - bias
        m = jnp.max(s, -1, keepdims=True)
        p = jnp.exp(s - m)
        return p, 1.0 / jnp.sum(p, -1, keepdims=True)

    p1, r1 = softmax_part(jnp.where(lane < E, q, zero))
    p2, r2 = softmax_part(jnp.where(lane >= E, q, zero))
    attn = p1 * r1 - p2 * (lam * r2)
    o = _dot(attn.astype(BF16), v)
    o = o * lax.rsqrt(jnp.mean(o * o, -1, keepdims=True) + LN_EPS) * subln_ref[...] * out_scale
    o_ref[0] = o.astype(o_ref.dtype)


def diff_attention_core(qkv, slopes, lam_full, subln, lambda_init, tq):
    B, S, D3 = qkv.shape
    D = D3 // 3
    H = D // (2 * HEAD_DIM)
    W = 2 * HEAD_DIM
    smem = pl.BlockSpec(memory_space=pltpu.SMEM)
    return pl.pallas_call(
        functools.partial(_diff_attn_kernel, tq=tq, out_scale=1.0 - lambda_init),
        grid=(B, H, S // tq),
        in_specs=[smem, smem,
                  pl.BlockSpec((1, tq, W), lambda b, h, i: (b, i, h)),
                  pl.BlockSpec((1, S, W), lambda b, h, i: (b, 0, H + h)),
                  pl.BlockSpec((1, S, W), lambda b, h, i: (b, 0, 2 * H + h)),
                  pl.BlockSpec((1, W), lambda b, h, i: (0, 0))],
        out_specs=pl.BlockSpec((1, tq, W), lambda b, h, i: (b, i, h)),
        out_shape=jax.ShapeDtypeStruct((B, S, D), BF16),
        compiler_params=_params("parallel", "parallel", "arbitrary"),
        name="diff_attention",
    )(slopes, lam_full, qkv, qkv, qkv, subln.reshape(1, W))


DIL_STEP_TOKENS = 2048


def _dil_attn_kernel(slope_ref, q_ref, k_ref, v_ref, o_ref, lse_ref, *, tq, half, dilation, nq):
    hp = pl.program_id(1)
    blk = pl.program_id(2)
    E = HEAD_DIM
    L = k_ref.shape[1] // dilation
    kw = tq + 2 * half
    lane = lax.broadcasted_iota(jnp.int32, (tq, 2 * E), 1)
    first = lane < E
    zero = jnp.zeros((tq, 2 * E), BF16)
    slope0 = slope_ref[2 * hp]
    slope1 = slope_ref[2 * hp + 1]
    row = lax.broadcasted_iota(jnp.int32, (tq, 1), 0)
    colk = lax.broadcasted_iota(jnp.int32, (1, kw), 1)

    def rows(start, size):
        return pl.ds(start, size) if dilation == 1 else pl.ds(start, size, stride=dilation)

    for j in range(nq):
        uq = (blk * nq + j) * tq
        ks = jnp.clip(uq - half, 0, L - kw)
        rel = jnp.abs((ks + colk) - (uq + row))
        valid = rel <= half
        dist = (dilation * rel).astype(F32)
        for r in range(dilation):
            local = rows(j * tq * dilation + r, tq)
            window = rows(ks * dilation + r, kw)
            q = (q_ref[0, local, :] * (E ** -0.5)).astype(BF16)
            k = k_ref[0, window, :].astype(BF16)
            v = v_ref[0, window, :].astype(BF16)

            def one_head(qc, slope):
                s = jnp.where(valid, _dot_nt(qc, k) - slope * dist, NEG_INF)
                m = jnp.max(s, -1, keepdims=True)
                p = jnp.exp(s - m)
                l = jnp.sum(p, -1, keepdims=True)
                o = _dot(p.astype(BF16), v) * (1.0 / l)
                return o, m + jnp.log(l)

            o0, lse0 = one_head(jnp.where(first, q, zero), slope0)
            o1, lse1 = one_head(jnp.where(first, zero, q), slope1)
            o_ref[0, local, :] = jnp.where(first, o0, o1)
            lse_ref[0, local, :] = jnp.where(first, lse0, lse1)


def dilated_group_core(qkv, slopes, g, window, dilation, tq):
    B, S, C = qkv.shape
    G = len(DIL_PATTERNS)
    D = C // (3 * G)
    W = 2 * HEAD_DIM
    HP = D // W
    half = window // (2 * dilation)
    base = g * 3 * HP
    nq = DIL_STEP_TOKENS // (tq * dilation)
    smem = pl.BlockSpec(memory_space=pltpu.SMEM)
    out_spec = pl.BlockSpec((1, DIL_STEP_TOKENS, W), lambda b, hp, i: (b, i, hp))
    o, lse = pl.pallas_call(
        functools.partial(_dil_attn_kernel, tq=tq, half=half, dilation=dilation, nq=nq),
        grid=(B, HP, S // DIL_STEP_TOKENS),
        in_specs=[smem,
                  pl.BlockSpec((1, DIL_STEP_TOKENS, W), lambda b, hp, i: (b, i, base + hp)),
                  pl.BlockSpec((1, S, W), lambda b, hp, i: (b, 0, base + HP + hp)),
                  pl.BlockSpec((1, S, W), lambda b, hp, i: (b, 0, base + 2 * HP + hp))],
        out_specs=[out_spec, out_spec],
        out_shape=[jax.ShapeDtypeStruct((B, S, D), F32)] * 2,
        compiler_params=_params("parallel", "parallel", "arbitrary"),
        name=f"dilated_attention_d{dilation}",
    )(slopes, qkv, qkv, qkv)
    return o.reshape(B * S, D), lse.reshape(B * S, D)


def _dil_merge_kernel(o0, o1, o2, l0, l1, l2, out_ref):
    a, b, c = l0[...], l1[...], l2[...]
    m = jnp.maximum(jnp.maximum(a, b), c)
    ea, eb, ec = jnp.exp(a - m), jnp.exp(b - m), jnp.exp(c - m)
    inv = 1.0 / (ea + eb + ec)
    out_ref[...] = ((ea * inv) * o0[...] + (eb * inv) * o1[...] + (ec * inv) * o2[...]).astype(out_ref.dtype)


def dilated_merge(outs, lses, tm):
    M, D = outs[0].shape
    spec = pl.BlockSpec((tm, D), lambda i: (i, 0))
    return pl.pallas_call(
        _dil_merge_kernel,
        grid=(M // tm,),
        in_specs=[spec] * 6,
        out_specs=spec,
        out_shape=jax.ShapeDtypeStruct((M, D), BF16),
        compiler_params=_params("parallel"),
        name="dilated_merge",
    )(*outs, *lses)


def _shifted_rows(x, prev_ref, next_ref, i, n_i, reverse):
    tm = x.shape[0]
    r = lax.broadcasted_iota(jnp.int32, (tm, 1), 0)
    if reverse:
        edge = jnp.where(i == n_i - 1, 0.0, next_ref[0, 0:1, :])
        return jnp.where(r == tm - 1, edge, pltpu.roll(x, tm - 1, 0))
    edge = jnp.where(i == 0, 0.0, prev_ref[0, 7:8, :])
    return jnp.where(r == 0, edge, pltpu.roll(x, 1, 0))


def _rwkv_proj_kernel(x_ref, prev_ref, next_ref, mu_ref, wr_ref, wk_ref, wv_ref, w0_ref, w1_ref, w2_ref,
                      a0_ref, a1_ref, a2_ref, g1_ref, g2_ref,
                      r_ref, k_ref, v_ref, lw_ref, a_ref, g_ref, *, reverse):
    i = pl.program_id(1)
    x = x_ref[0]
    xx = _shifted_rows(x, prev_ref, next_ref, i, pl.num_programs(1), reverse) - x
    mu = mu_ref[...]

    def mix(j):
        return (x + xx * mu[j:j + 1]).astype(BF16)

    r_ref[0] = _dot(mix(0), wr_ref[...])
    k_ref[0] = _dot(mix(2), wk_ref[...])
    v_ref[0] = _dot(mix(3), wv_ref[...])
    z = -(w0_ref[...] + _dot(jnp.tanh(_dot(mix(1), w1_ref[...])).astype(BF16), w2_ref[...]))
    softplus = jnp.maximum(z, 0.0) + jnp.log(1.0 + jnp.exp(-jnp.abs(z)))
    lw_ref[0] = -jnp.exp(-softplus - 0.5)
    a_ref[0] = jax.nn.sigmoid(a0_ref[...] + _dot(_dot(mix(4), a1_ref[...]).astype(BF16), a2_ref[...]))
    g_ref[0] = _dot(jax.nn.sigmoid(_dot(mix(5), g1_ref[...])).astype(BF16), g2_ref[...])


def rwkv_projections(x, mu, w_rkv, w0, w1, w2, a0, a1, a2, g1, g2, reverse, tm):
    B, S, D = x.shape
    nb = tm // 8
    fixed2 = lambda b, i: (0, 0)
    tile = pl.BlockSpec((1, tm, D), lambda b, i: (b, i, 0))
    full = lambda a: pl.BlockSpec(a.shape, fixed2)
    ops = (mu, w_rkv[0], w_rkv[1], w_rkv[2], w0.reshape(1, D), w1, w2, a0.reshape(1, D), a1, a2, g1, g2)
    return pl.pallas_call(
        functools.partial(_rwkv_proj_kernel, reverse=reverse),
        grid=(B, S // tm),
        in_specs=[tile,
                  pl.BlockSpec((1, 8, D), lambda b, i: (b, jnp.maximum(i * nb - 1, 0), 0)),
                  pl.BlockSpec((1, 8, D), lambda b, i: (b, jnp.minimum((i + 1) * nb, S // 8 - 1), 0))]
        + [full(a) for a in ops],
        out_specs=[tile] * 6,
        out_shape=[jax.ShapeDtypeStruct((B, S, D), F32)] * 6,
        compiler_params=_params("parallel", "arbitrary"),
        name="rwkv_projections_bwd" if reverse else "rwkv_projections_fwd",
    )(x, x, x, *ops)


SLAB_HEADS = 4
SLAB = SLAB_HEADS * HEAD_DIM


def _split3(x):
    hi = x.astype(BF16)
    r1 = x - hi.astype(F32)
    mid = r1.astype(BF16)
    lo = (r1 - mid.astype(F32)).astype(BF16)
    return hi, mid, lo


def _block_diag(x, block_mask):
    xb = x.astype(BF16)
    return jnp.where(block_mask, jnp.concatenate([xb] * SLAB_HEADS, 0), jnp.zeros((), BF16))


def _per_head_dot(lhs, x, block_mask, nt=False):
    mm = _dot_nt if nt else _dot
    return mm(lhs.astype(BF16), _block_diag(x, block_mask))


def _per_head_sum(x, ones_bd):
    M = x.shape[0]
    s = _dot(jnp.concatenate(_split3(x), 0), ones_bd)
    return s[:M] + s[M:2 * M] + s[2 * M:]


def _wkv_slab_chunk(state, r, k, v, cum, lw, a, kkw, kaw, rkw, gng, gnb, gate, reverse, consts):
    block_mask, ones_bd, strict, incl, eye, lane_head = consts
    C = r.shape[0]
    kk = k * kkw
    k = k * (1.0 + (a - 1.0) * kaw)
    sums = _per_head_sum(jnp.concatenate([kk * kk, r * k * rkw], 0), ones_bd)
    yield
    kk = kk * lax.rsqrt(jnp.maximum(sums[:C], 1e-24))
    bonus = sums[C:]
    dec = jnp.exp(cum)
    inv = jnp.exp(-cum)
    alpha_bar = -kk * jnp.exp(cum - lw)
    beta_t = kk * a * inv
    k_t = k * inv
    r_bar = r * dec

    ar = jnp.concatenate([alpha_bar, r_bar], 0)
    gb = _per_head_dot(ar, beta_t, block_mask, nt=True)
    gk = _per_head_dot(ar, k_t, block_mask, nt=True)
    yield
    a_ab = jnp.where(strict, gb[:C], 0.0)
    a_rb = jnp.where(incl, gb[C:], 0.0)
    a_ak = jnp.where(strict, gk[:C], 0.0)
    a_rk = jnp.where(incl, gk[C:], 0.0)

    x = eye + a_ab
    p = _per_head_dot(a_ab, a_ab, block_mask)
    av = _per_head_dot(jnp.concatenate([a_ak, a_rk], 0), v, block_mask)
    ss = _per_head_dot(ar, state, block_mask, nt=True)
    yield
    m = 2
    while 2 * m < C:
        xp = _per_head_dot(jnp.concatenate([x, p], 0), p, block_mask)
        yield
        x = x + xp[:C]
        p = xp[C:]
        m *= 2
    xp = _per_head_dot(x, p, block_mask)
    yield
    x = x + xp
    u = _per_head_dot(x, ss[:C] + av[:C], block_mask)
    yield
    y = ss[C:] + _per_head_dot(a_rb, u, block_mask) + av[C:]

    total = cum[0:1] if reverse else cum[C - 1:C]
    end = jnp.exp(total)
    cross = lax.dot_general(jnp.concatenate([u, v], 0).astype(BF16),
                            jnp.concatenate([beta_t * end, k_t * end], 0).astype(BF16),
                            TN_DIMS, preferred_element_type=F32)
    mu = _per_head_sum(y, ones_bd) * (1.0 / HEAD_DIM)
    yield
    new_state = state * end
    for hh in range(SLAB_HEADS):
        new_state = new_state + jnp.where(lane_head == hh, cross[hh * HEAD_DIM:(hh + 1) * HEAD_DIM], 0.0)

    yc = y - mu
    var = _per_head_sum(yc * yc, ones_bd) * (1.0 / HEAD_DIM)
    yield
    out = yc * lax.rsqrt(var + GN_EPS) * gng + gnb + bonus * v
    return new_state, out * gate


def _wkv_kernel(rf, kf, vf, lwf, af, gf, rb, kb, vb, lwb, ab, gb,
                kk_ref, ka_ref, rk_ref, gng_ref, gnb_ref, of_ref, ob_ref, state_ref):
    c = pl.program_id(1)
    C = rf.shape[1]
    D = rf.shape[2]
    N = HEAD_DIM

    @pl.when(c == 0)
    def _():
        state_ref[...] = jnp.zeros_like(state_ref)

    row = lax.broadcasted_iota(jnp.int32, (SLAB, SLAB), 0)
    col = lax.broadcasted_iota(jnp.int32, (SLAB, SLAB), 1)
    block_mask = (row // N) == (col // N)
    ones_bd = jnp.where(block_mask, 1.0, 0.0).astype(BF16)
    t = lax.broadcasted_iota(jnp.int32, (C, SLAB), 0)
    lane = lax.broadcasted_iota(jnp.int32, (C, SLAB), 1)
    s = lane % C
    lane_head = lax.broadcasted_iota(jnp.int32, (N, SLAB), 1) // N
    ti = lax.broadcasted_iota(jnp.int32, (C, C), 0)
    si = lax.broadcasted_iota(jnp.int32, (C, C), 1)
    eye = (s == t).astype(F32)

    chains = []
    for d, (refs, o_ref) in enumerate((((rf, kf, vf, lwf, af, gf), of_ref), ((rb, kb, vb, lwb, ab, gb), ob_ref))):
        reverse = d == 1
        strict = (s > t) if reverse else (s < t)
        incl = (s >= t) if reverse else (s <= t)
        consts = (block_mask, ones_bd, strict, incl, eye, lane_head)
        tri = ((si >= ti) if reverse else (si <= ti)).astype(BF16)
        lw_all = refs[3][0]
        parts = _dot(tri, jnp.concatenate(_split3(lw_all), 1))
        cum_all = parts[:, :D] + parts[:, D:2 * D] + parts[:, 2 * D:]
        for sb in range(D // SLAB):
            sl = slice(sb * SLAB, (sb + 1) * SLAB)
            r, k, v, lw, a, gate = (ref[0, :, sl] for ref in refs)
            chains.append((d, sb, sl, o_ref, _wkv_slab_chunk(
                state_ref[d, sb], r, k, v, cum_all[:, sl], lw, a,
                kk_ref[d:d + 1, sl], ka_ref[d:d + 1, sl], rk_ref[:, sl], gng_ref[:, sl], gnb_ref[:, sl],
                gate, reverse, consts)))

    done = False
    while not done:
        for d, sb, sl, o_ref, chain in chains:
            try:
                next(chain)
            except StopIteration as stop:
                new_state, out = stop.value
                state_ref[d, sb] = new_state
                o_ref[0, :, sl] = out
                done = True


def rwkv_scan(fwd, bwd, k_k, k_a, r_k, gn_g, gn_b):
    B, S, D = fwd[0].shape
    C = WKV_CHUNK
    NC = S // C
    f_spec = pl.BlockSpec((1, C, D), lambda b, c: (b, c, 0))
    b_spec = pl.BlockSpec((1, C, D), lambda b, c: (b, NC - 1 - c, 0))
    two = pl.BlockSpec((2, D), lambda b, c: (0, 0))
    one = pl.BlockSpec((1, D), lambda b, c: (0, 0))
    return pl.pallas_call(
        _wkv_kernel,
        grid=(B, NC),
        in_specs=[f_spec] * 6 + [b_spec] * 6 + [two, two, one, one, one],
        out_specs=[f_spec, b_spec],
        out_shape=[jax.ShapeDtypeStruct((B, S, D), F32)] * 2,
        scratch_shapes=[pltpu.VMEM((2, D // SLAB, HEAD_DIM, SLAB), F32)],
        compiler_params=_params("parallel", "arbitrary"),
        name="rwkv_scan",
    )(*fwd, *bwd, k_k, k_a, r_k.reshape(1, D), gn_g.reshape(1, D), gn_b.reshape(1, D))


def _xatt_kernel(x_ref, kv_ref, wq_ref, wo_ref, g_ref, b_ref, o_ref, *, alpha):
    x = x_ref[0]
    D = x.shape[-1]
    E = D // XATT_HEADS
    q = (_dot(x.astype(BF16), wq_ref[...]) * (E ** -0.5)).astype(BF16)
    outs = []
    for h in range(XATT_HEADS):
        kh = kv_ref[0, :, h * E:(h + 1) * E]
        vh = kv_ref[0, :, D + h * E:D + (h + 1) * E]
        s = _dot_nt(q[:, h * E:(h + 1) * E], kh)
        p = jnp.exp(s - jnp.max(s, -1, keepdims=True))
        inv = 1.0 / jnp.sum(p, -1, keepdims=True)
        outs.append((_dot(p.astype(BF16), vh) * inv).astype(BF16))
    y = alpha * x + _dot(jnp.concatenate(outs, -1), wo_ref[...])
    o_ref[0] = _layer_norm(y, g_ref[...], b_ref[...])


def cross_attention_sublayer(x, kv, w_q, w_o, g, b, alpha, tm):
    B, S, D = x.shape
    M = kv.shape[1]
    fixed = lambda bb, i: (0, 0)
    tile = pl.BlockSpec((1, tm, D), lambda bb, i: (bb, i, 0))
    return pl.pallas_call(
        functools.partial(_xatt_kernel, alpha=alpha),
        grid=(B, S // tm),
        in_specs=[tile, pl.BlockSpec((1, M, 2 * D), lambda bb, i: (bb, 0, 0)),
                  pl.BlockSpec((D, D), fixed), pl.BlockSpec((D, D), fixed),
                  pl.BlockSpec((1, D), fixed), pl.BlockSpec((1, D), fixed)],
        out_specs=tile,
        out_shape=jax.ShapeDtypeStruct((B, S, D), F32),
        compiler_params=_params("parallel", "arbitrary"),
        name="cross_attention",
    )(x, kv, w_q, w_o, g.reshape(1, D), b.reshape(1, D))


def _ffn_in_kernel(x_ref, prev_ref, next_ref, wg_ref, wv_ref, cw_ref, cb_ref, o_ref):
    i = pl.program_id(1)
    n_i = pl.num_programs(1)
    x = x_ref[0].astype(BF16)
    tm = x.shape[0]
    wg = wg_ref[...]
    gate = _dot(x, wg)
    val = _dot(x, wv_ref[...])
    g_prev = _dot(prev_ref[0].astype(BF16), wg)[7:8]
    g_next = _dot(next_ref[0].astype(BF16), wg)[0:1]
    g_prev = jnp.where(i == 0, 0.0, g_prev)
    g_next = jnp.where(i == n_i - 1, 0.0, g_next)
    r = lax.broadcasted_iota(jnp.int32, (tm, 1), 0)
    before = jnp.where(r == 0, g_prev, pltpu.roll(gate, 1, 0))
    after = jnp.where(r == tm - 1, g_next, pltpu.roll(gate, tm - 1, 0))
    cw = cw_ref[...]
    conv = before * cw[0:1] + gate * cw[1:2] + after * cw[2:3] + cb_ref[...]
    gelu = 0.5 * conv * (1.0 + lax.erf(conv * (2.0 ** -0.5)))
    o_ref[0] = (gelu * val).astype(o_ref.dtype)


def ffn_in(x, w_in, conv_w, conv_b, tm, tn):
    B, S, D = x.shape
    F = w_in.shape[1] // 2
    nb = tm // 8
    nj = F // tn
    return pl.pallas_call(
        _ffn_in_kernel,
        grid=(B, S // tm, nj),
        in_specs=[pl.BlockSpec((1, tm, D), lambda b, i, j: (b, i, 0)),
                  pl.BlockSpec((1, 8, D), lambda b, i, j: (b, jnp.maximum(i * nb - 1, 0), 0)),
                  pl.BlockSpec((1, 8, D), lambda b, i, j: (b, jnp.minimum((i + 1) * nb, S // 8 - 1), 0)),
                  pl.BlockSpec((D, tn), lambda b, i, j: (0, j)),
                  pl.BlockSpec((D, tn), lambda b, i, j: (0, nj + j)),
                  pl.BlockSpec((3, tn), lambda b, i, j: (0, j)),
                  pl.BlockSpec((1, tn), lambda b, i, j: (0, j))],
        out_specs=pl.BlockSpec((1, tm, tn), lambda b, i, j: (b, i, j)),
        out_shape=jax.ShapeDtypeStruct((B, S, F), BF16),
        compiler_params=_params("parallel", "parallel", "arbitrary"),
        name="ffn_in",
    )(x, x, x, w_in, w_in, conv_w, conv_b.reshape(1, F))


def _alibi_slopes(n):
    return jnp.exp2(-8.0 * jnp.arange(1, n + 1, dtype=F32) / n)


def kernel(x, mem, diff_w_qkv, diff_lambda, diff_subln, diff_w_o, dil_w_qkv, dil_w_o, rwkv_mu, rwkv_w_rkv, rwkv_w0, rwkv_w1, rwkv_w2, rwkv_a0, rwkv_a1, rwkv_a2, rwkv_g1, rwkv_g2, rwkv_k_k, rwkv_k_a, rwkv_r_k, rwkv_gn_g, rwkv_gn_b, rwkv_w_o, xatt_w_q, xatt_w_kv, xatt_w_o, ffn_w_in, ffn_conv_w, ffn_conv_b, ffn_w_out, ln_g, ln_b):
    B, S, D = x.shape
    depth = xatt_w_q.shape[0]
    alpha = (2 * depth) ** 0.25
    T = B * S
    bf = lambda a: a.astype(BF16)
    mem2 = mem.reshape(-1, D)

    for i in range(depth):
        m, j = i % N_MIXERS, i // N_MIXERS
        x2 = x.reshape(T, D)
        if m == 0:
            lambda_init = 0.8 - 0.6 * math.exp(-0.3 * i)
            lamf = diff_lambda[j].astype(F32)
            lam_full = (jnp.exp(jnp.sum(lamf[0] * lamf[1])) - jnp.exp(jnp.sum(lamf[2] * lamf[3]))
                        + lambda_init).reshape(1)
            qkv = matmul(x2, bf(diff_w_qkv[j]), BF16, tm=1024, tn=512).reshape(B, S, 3 * D)
            o = diff_attention_core(qkv, _alibi_slopes(D // (2 * HEAD_DIM)), lam_full, diff_subln[j],
                                    lambda_init, tq=256)
            hs, w_o = [o.reshape(T, D)], diff_w_o[j]
        elif m == 1:
            qkv = matmul(x2, bf(dil_w_qkv[j]), F32, tm=1024, tn=512).reshape(B, S, -1)
            slopes = _alibi_slopes(D // HEAD_DIM)
            parts = [dilated_group_core(qkv, slopes, g, window, dilation, tq=128)
                     for g, (window, dilation) in enumerate(DIL_PATTERNS)]
            hs = [dilated_merge([p[0] for p in parts], [p[1] for p in parts], tm=512)]
            w_o = dil_w_o[j]
        else:
            dirs = [rwkv_projections(x, rwkv_mu[j, d], bf(rwkv_w_rkv[j]), rwkv_w0[j, d], bf(rwkv_w1[j, d]),
                                     bf(rwkv_w2[j, d]), rwkv_a0[j, d], bf(rwkv_a1[j, d]), bf(rwkv_a2[j, d]),
                                     bf(rwkv_g1[j, d]), bf(rwkv_g2[j, d]), reverse=d == 1, tm=256)
                    for d in range(2)]
            yf, yb = rwkv_scan(dirs[0], dirs[1], rwkv_k_k[j], rwkv_k_a[j], rwkv_r_k[j],
                               rwkv_gn_g[j], rwkv_gn_b[j])
            hs, w_o = [yf.reshape(T, D), yb.reshape(T, D)], rwkv_w_o[j]
        x2 = matmul_residual_ln(hs, bf(w_o), x2, ln_g[i, 0], ln_b[i, 0], alpha, tm=512)

        kv = matmul(mem2, bf(xatt_w_kv[i]), BF16, tm=512, tn=512).reshape(B, -1, 2 * D)
        x = cross_attention_sublayer(x2.reshape(B, S, D), kv, bf(xatt_w_q[i]), bf(xatt_w_o[i]),
                                     ln_g[i, 1], ln_b[i, 1], alpha, tm=512)

        act = ffn_in(x, bf(ffn_w_in[i]), ffn_conv_w[i], ffn_conv_b[i], tm=512, tn=1408)
        x = matmul_residual_ln([act.reshape(T, -1)], bf(ffn_w_out[i]), x.reshape(T, D),
                               ln_g[i, 2], ln_b[i, 2], alpha, tm=512).reshape(B, S, D)
    return x
```

```python
import functools
import math

import jax
import jax.numpy as jnp
from jax import lax
from jax.experimental import pallas as pl
from jax.experimental.pallas import tpu as pltpu

BF16 = jnp.bfloat16
F32 = jnp.float32

HEAD_DIM = 64
N_MIXERS = 3
DIL_PATTERNS = ((128, 1), (512, 4), (2048, 16))
XATT_HEADS = 4
LN_EPS = 1e-5
GN_EPS = 64e-5
NEG_INF = -1e30
WKV_CHUNK = 64
VMEM_LIMIT_BYTES = 56 * 1024 * 1024
LOG2E = math.log2(math.e)

NT_DIMS = (((1,), (1,)), ((), ()))
TN_DIMS = (((0,), (0,)), ((), ()))


def _params(*semantics):
    return pltpu.CompilerParams(dimension_semantics=semantics, vmem_limit_bytes=VMEM_LIMIT_BYTES)


def _dot(a, b):
    return jnp.dot(a, b, preferred_element_type=F32)


def _dot_nt(a, b, precision=None):
    return lax.dot_general(a, b, NT_DIMS, preferred_element_type=F32, precision=precision)


def _layer_norm(y, g, b):
    mu = jnp.mean(y, -1, keepdims=True)
    yc = y - mu
    var = jnp.mean(yc * yc, -1, keepdims=True)
    return yc * lax.rsqrt(var + LN_EPS) * g + b


def _run_pipelined(chains, skew):
    results = [None] * len(chains)
    live = [True] * len(chains)
    tick = 0
    while any(live):
        for n, chain in enumerate(chains):
            if live[n] and tick >= n * skew:
                try:
                    next(chain)
                except StopIteration as stop:
                    results[n] = stop.value
                    live[n] = False
        tick += 1
    return results


def _mm_kernel(x_ref, w_ref, o_ref):
    o_ref[...] = _dot(x_ref[...].astype(BF16), w_ref[...]).astype(o_ref.dtype)


def _mm_scaled_kernel(x_ref, w_ref, scale_ref, o_ref):
    o_ref[...] = (_dot(x_ref[...].astype(BF16), w_ref[...]) * scale_ref[...]).astype(o_ref.dtype)


def matmul(x, w, out_dtype, tm, tn, col_scale=None):
    M, K = x.shape
    N = w.shape[1]
    in_specs = [pl.BlockSpec((tm, K), lambda i, j: (i, 0)), pl.BlockSpec((K, tn), lambda i, j: (0, j))]
    operands = [x, w]
    if col_scale is not None:
        in_specs.append(pl.BlockSpec((1, tn), lambda i, j: (0, j)))
        operands.append(col_scale.reshape(1, N))
    return pl.pallas_call(
        _mm_kernel if col_scale is None else _mm_scaled_kernel,
        grid=(M // tm, N // tn),
        in_specs=in_specs,
        out_specs=pl.BlockSpec((tm, tn), lambda i, j: (i, j)),
        out_shape=jax.ShapeDtypeStruct((M, N), out_dtype),
        compiler_params=_params("parallel", "arbitrary"),
        name="matmul",
    )(*operands)


ROW_CHUNK = 256


def _mm_res_ln_kernel(*refs, n_h, alpha):
    h_refs = refs[:n_h]
    w_ref, x_ref, g_ref, b_ref, o_ref = refs[n_h:]

    def row_chunk(r0):
        rows = slice(r0, r0 + ROW_CHUNK)
        h = h_refs[0][rows, :]
        for r in h_refs[1:]:
            h = h + r[rows, :]
        hw = _dot(h.astype(BF16), w_ref[...])
        yield
        o_ref[rows, :] = _layer_norm(alpha * x_ref[rows, :] + hw, g_ref[...], b_ref[...])

    _run_pipelined([row_chunk(r0) for r0 in range(0, o_ref.shape[0], ROW_CHUNK)], skew=1)


def matmul_residual_ln(hs, w, x, g, b, alpha, tm):
    M, K = hs[0].shape
    D = w.shape[1]
    row = lambda i: (i, 0)
    fixed = lambda i: (0, 0)
    return pl.pallas_call(
        functools.partial(_mm_res_ln_kernel, n_h=len(hs), alpha=alpha),
        grid=(M // tm,),
        in_specs=[pl.BlockSpec((tm, K), row) for _ in hs]
        + [pl.BlockSpec((K, D), fixed), pl.BlockSpec((tm, D), row),
           pl.BlockSpec((1, D), fixed), pl.BlockSpec((1, D), fixed)],
        out_specs=pl.BlockSpec((tm, D), row),
        out_shape=jax.ShapeDtypeStruct((M, D), F32),
        compiler_params=_params("parallel"),
        name="matmul_residual_ln",
    )(*hs, w, x, g.reshape(1, D), b.reshape(1, D))


ALIBI_SPLIT = 3
ALIBI_FEATS = 4 * ALIBI_SPLIT


def _alibi_features(pos, lane0, cs, key_side):
    lane = lax.broadcasted_iota(jnp.int32, (1, 2 * HEAD_DIM), 1) - lane0
    kind = lane & 3
    piece = lane >> 2
    live = (lane >= 0) & (lane < ALIBI_FEATS)
    c = jnp.where(piece == 0, cs[0], jnp.where(piece == 1, cs[1], cs[2]))
    hi = (pos >> 6).astype(F32)
    lo = (pos & 63).astype(F32)
    if key_side:
        f = jnp.where(kind == 0, -64.0 * c, jnp.where(kind == 1, -c, jnp.where(kind == 2, 64.0 * hi, lo)))
    else:
        f = jnp.where(kind == 0, hi, jnp.where(kind == 1, lo, c))
    return jnp.where(live, f, 0.0)


def _diff_attn_kernel(cs_ref, lam_ref, q_ref, k_ref, v_ref, subln_ref, o_ref, kaug_ref, vaug_ref, s_ref,
                      *, tq, out_scale):
    h = pl.program_id(1)
    qi = pl.program_id(2)
    E = HEAD_DIM
    S = k_ref.shape[1]
    nb = S // tq
    cs = [cs_ref[ALIBI_SPLIT * h + m] for m in range(ALIBI_SPLIT)]
    lane = lax.broadcasted_iota(jnp.int32, (1, 2 * E), 1)
    first = lane < E

    @pl.when(qi == 0)
    def _():
        k = k_ref[0]
        pos = lax.broadcasted_iota(jnp.int32, (S, 1), 0)
        for c, lane0 in enumerate((E, 0)):
            feat = _alibi_features(pos, lane0, cs, key_side=True)
            own = first if c == 0 else jnp.logical_not(first)
            kaug_ref[c, 0] = jnp.where(own, k, feat.astype(BF16))
            kaug_ref[c, 1] = jnp.where(own, k, (-feat).astype(BF16))
        ones = jnp.where(lane == 0, 1.0, 0.0).astype(BF16)
        vaug_ref[...] = jnp.concatenate([v_ref[0], jnp.broadcast_to(ones, (S, 2 * E))], 1)

    start = pl.multiple_of(qi * tq, tq)
    half = tq // 2

    def block_max(t):
        bm = t[:, :2 * E]
        for g in range(1, tq // (2 * E)):
            bm = jnp.maximum(bm, t[:, g * 2 * E:(g + 1) * 2 * E])
        return bm

    def softmax_map(c, r0):
        rows = slice(r0, r0 + half)
        own = first if c == 0 else jnp.logical_not(first)
        pos_q = qi * tq + r0 + lax.broadcasted_iota(jnp.int32, (half, 1), 0)
        q = q_ref[0, rows, :]
        qa = jnp.where(own, q, _alibi_features(pos_q, E if c == 0 else 0, cs, key_side=False).astype(BF16))
        lane_max = jnp.full((half, 2 * E), NEG_INF, F32)
        for jb in range(nb):
            side = (jb > qi).astype(jnp.int32)
            t = _dot_nt(qa, kaug_ref[c, side, jb * tq:(jb + 1) * tq, :])
            s_ref[c, jb, rows, :] = t
            lane_max = jnp.maximum(lane_max, jnp.where(jb == qi, NEG_INF, block_max(t)))
            yield
        t = jnp.minimum(s_ref[c, qi, rows, :], _dot_nt(qa, kaug_ref[c, 1, pl.ds(start, tq), :]))
        s_ref[c, qi, rows, :] = t
        m = jnp.max(jnp.maximum(lane_max, block_max(t)), -1, keepdims=True)
        yield
        ps = []
        for jb in range(nb):
            ps.append(jnp.exp2(s_ref[c, jb, rows, :] - m).astype(BF16))
            yield
        ol = jnp.zeros((half, 4 * E), F32)
        for jb in range(nb):
            ol = ol + _dot(ps[jb], vaug_ref[jb * tq:(jb + 1) * tq, :])
            yield
        return ol[:, :2 * E] * (1.0 / ol[:, 2 * E:2 * E + 1])

    items = [(c, r0) for r0 in (0, half) for c in (0, 1)]
    results = _run_pipelined([softmax_map(c, r0) for c, r0 in items], skew=nb + 1)
    for n, r0 in enumerate((0, half)):
        o = results[2 * n] - lam_ref[0] * results[2 * n + 1]
        o = o * lax.rsqrt(jnp.mean(o * o, -1, keepdims=True) + LN_EPS) * subln_ref[...] * out_scale
        o_ref[0, r0:r0 + half, :] = o.astype(o_ref.dtype)


def diff_attention_core(qkv, slope_pieces, lam_full, subln, lambda_init, tq):
    B, S, D3 = qkv.shape
    D = D3 // 3
    H = D // (2 * HEAD_DIM)
    W = 2 * HEAD_DIM
    smem = pl.BlockSpec(memory_space=pltpu.SMEM)
    return pl.pallas_call(
        functools.partial(_diff_attn_kernel, tq=tq, out_scale=1.0 - lambda_init),
        grid=(B, H, S // tq),
        in_specs=[smem, smem,
                  pl.BlockSpec((1, tq, W), lambda b, h, i: (b, i, h)),
                  pl.BlockSpec((1, S, W), lambda b, h, i: (b, 0, H + h)),
                  pl.BlockSpec((1, S, W), lambda b, h, i: (b, 0, 2 * H + h)),
                  pl.BlockSpec((1, W), lambda b, h, i: (0, 0))],
        out_specs=pl.BlockSpec((1, tq, W), lambda b, h, i: (b, i, h)),
        out_shape=jax.ShapeDtypeStruct((B, S, D), BF16),
        scratch_shapes=[pltpu.VMEM((2, 2, S, W), BF16), pltpu.VMEM((S, 2 * W), BF16),
                        pltpu.VMEM((2, S // tq, tq, tq), F32)],
        compiler_params=_params("parallel", "parallel", "arbitrary"),
        name="diff_attention",
    )(slope_pieces, lam_full, qkv, qkv, qkv, subln.reshape(1, W))


def _split_slopes(slopes):
    pieces, rest = [], slopes
    for _ in range(ALIBI_SPLIT):
        piece = rest.astype(BF16).astype(F32)
        pieces.append(piece)
        rest = rest - piece
    return jnp.stack(pieces, -1).reshape(-1)


DIL_STEP_TOKENS = 2048


def _dil_attn_kernel(slope_ref, q_ref, k_ref, v_ref, o_ref, lse_ref, *, tq, half, dilation, nq):
    hp = pl.program_id(1)
    blk = pl.program_id(2)
    E = HEAD_DIM
    L = k_ref.shape[1] // dilation
    kw = tq + 2 * half
    lane = lax.broadcasted_iota(jnp.int32, (tq, 2 * E), 1)
    first = lane < E
    zero = jnp.zeros((tq, 2 * E), BF16)
    slope0 = slope_ref[2 * hp]
    slope1 = slope_ref[2 * hp + 1]
    row = lax.broadcasted_iota(jnp.int32, (tq, 1), 0)
    colk = lax.broadcasted_iota(jnp.int32, (1, kw), 1)

    def rows(start, size):
        return pl.ds(start, size) if dilation == 1 else pl.ds(start, size, stride=dilation)

    for j in range(nq):
        uq = (blk * nq + j) * tq
        ks = jnp.clip(uq - half, 0, L - kw)
        rel = jnp.abs((ks + colk) - (uq + row))
        valid = rel <= half
        dist = (dilation * rel).astype(F32)
        for r in range(dilation):
            local = rows(j * tq * dilation + r, tq)
            window = rows(ks * dilation + r, kw)
            q = (q_ref[0, local, :] * (E ** -0.5)).astype(BF16)
            k = k_ref[0, window, :].astype(BF16)
            v = v_ref[0, window, :].astype(BF16)

            def one_head(qc, slope):
                s = jnp.where(valid, _dot_nt(qc, k) - slope * dist, NEG_INF)
                m = jnp.max(s, -1, keepdims=True)
                p = jnp.exp(s - m)
                l = jnp.sum(p, -1, keepdims=True)
                o = _dot(p.astype(BF16), v) * (1.0 / l)
                return o, m + jnp.log(l)

            o0, lse0 = one_head(jnp.where(first, q, zero), slope0)
            o1, lse1 = one_head(jnp.where(first, zero, q), slope1)
            o_ref[0, local, :] = jnp.where(first, o0, o1)
            lse_ref[0, local, :] = jnp.where(first, lse0, lse1)


def dilated_group_core(qkv, slopes, g, window, dilation, tq):
    B, S, C = qkv.shape
    G = len(DIL_PATTERNS)
    D = C // (3 * G)
    W = 2 * HEAD_DIM
    HP = D // W
    half = window // (2 * dilation)
    base = g * 3 * HP
    nq = DIL_STEP_TOKENS // (tq * dilation)
    smem = pl.BlockSpec(memory_space=pltpu.SMEM)
    out_spec = pl.BlockSpec((1, DIL_STEP_TOKENS, W), lambda b, hp, i: (b, i, hp))
    o, lse = pl.pallas_call(
        functools.partial(_dil_attn_kernel, tq=tq, half=half, dilation=dilation, nq=nq),
        grid=(B, HP, S // DIL_STEP_TOKENS),
        in_specs=[smem,
                  pl.BlockSpec((1, DIL_STEP_TOKENS, W), lambda b, hp, i: (b, i, base + hp)),
                  pl.BlockSpec((1, S, W), lambda b, hp, i: (b, 0, base + HP + hp)),
                  pl.BlockSpec((1, S, W), lambda b, hp, i: (b, 0, base + 2 * HP + hp))],
        out_specs=[out_spec, out_spec],
        out_shape=[jax.ShapeDtypeStruct((B, S, D), F32)] * 2,
        compiler_params=_params("parallel", "parallel", "arbitrary"),
        name=f"dilated_attention_d{dilation}",
    )(slopes, qkv, qkv, qkv)
    return o.reshape(B * S, D), lse.reshape(B * S, D)


def _dil_merge_kernel(o0, o1, o2, l0, l1, l2, out_ref):
    a, b, c = l0[...], l1[...], l2[...]
    m = jnp.maximum(jnp.maximum(a, b), c)
    ea, eb, ec = jnp.exp(a - m), jnp.exp(b - m), jnp.exp(c - m)
    inv = 1.0 / (ea + eb + ec)
    out_ref[...] = ((ea * inv) * o0[...] + (eb * inv) * o1[...] + (ec * inv) * o2[...]).astype(out_ref.dtype)


def dilated_merge(outs, lses, tm):
    M, D = outs[0].shape
    spec = pl.BlockSpec((tm, D), lambda i: (i, 0))
    return pl.pallas_call(
        _dil_merge_kernel,
        grid=(M // tm,),
        in_specs=[spec] * 6,
        out_specs=spec,
        out_shape=jax.ShapeDtypeStruct((M, D), BF16),
        compiler_params=_params("parallel"),
        name="dilated_merge",
    )(*outs, *lses)


def _shifted_rows(x, prev_ref, next_ref, i, n_i, reverse):
    tm = x.shape[0]
    r = lax.broadcasted_iota(jnp.int32, (tm, 1), 0)
    if reverse:
        edge = jnp.where(i == n_i - 1, 0.0, next_ref[0, 0:1, :])
        return jnp.where(r == tm - 1, edge, pltpu.roll(x, tm - 1, 0))
    edge = jnp.where(i == 0, 0.0, prev_ref[0, 7:8, :])
    return jnp.where(r == 0, edge, pltpu.roll(x, 1, 0))


def _rwkv_proj_kernel(x_ref, prev_ref, next_ref, mu_ref, wr_ref, wk_ref, wv_ref, w0_ref, w1_ref, w2_ref,
                      a0_ref, a1_ref, a2_ref, g1_ref, g2_ref,
                      r_ref, k_ref, v_ref, lw_ref, a_ref, g_ref, *, reverse):
    i = pl.program_id(1)
    x = x_ref[0]
    xx = _shifted_rows(x, prev_ref, next_ref, i, pl.num_programs(1), reverse) - x
    mu = mu_ref[...]

    def mix(j):
        return (x + xx * mu[j:j + 1]).astype(BF16)

    r_ref[0] = _dot(mix(0), wr_ref[...])
    k_ref[0] = _dot(mix(2), wk_ref[...])
    v_ref[0] = _dot(mix(3), wv_ref[...])
    z = -(w0_ref[...] + _dot(jnp.tanh(_dot(mix(1), w1_ref[...])).astype(BF16), w2_ref[...]))
    softplus = jnp.maximum(z, 0.0) + jnp.log(1.0 + jnp.exp(-jnp.abs(z)))
    lw_ref[0] = -jnp.exp(-softplus - 0.5)
    a_ref[0] = jax.nn.sigmoid(a0_ref[...] + _dot(_dot(mix(4), a1_ref[...]).astype(BF16), a2_ref[...]))
    g_ref[0] = _dot(jax.nn.sigmoid(_dot(mix(5), g1_ref[...])).astype(BF16), g2_ref[...])


def rwkv_projections(x, mu, w_rkv, w0, w1, w2, a0, a1, a2, g1, g2, reverse, tm):
    B, S, D = x.shape
    nb = tm // 8
    fixed2 = lambda b, i: (0, 0)
    tile = pl.BlockSpec((1, tm, D), lambda b, i: (b, i, 0))
    full = lambda a: pl.BlockSpec(a.shape, fixed2)
    ops = (mu, w_rkv[0], w_rkv[1], w_rkv[2], w0.reshape(1, D), w1, w2, a0.reshape(1, D), a1, a2, g1, g2)
    return pl.pallas_call(
        functools.partial(_rwkv_proj_kernel, reverse=reverse),
        grid=(B, S // tm),
        in_specs=[tile,
                  pl.BlockSpec((1, 8, D), lambda b, i: (b, jnp.maximum(i * nb - 1, 0), 0)),
                  pl.BlockSpec((1, 8, D), lambda b, i: (b, jnp.minimum((i + 1) * nb, S // 8 - 1), 0))]
        + [full(a) for a in ops],
        out_specs=[tile] * 6,
        out_shape=[jax.ShapeDtypeStruct((B, S, D), F32)] * 6,
        compiler_params=_params("parallel", "arbitrary"),
        name="rwkv_projections_bwd" if reverse else "rwkv_projections_fwd",
    )(x, x, x, *ops)


SLAB_HEADS = 4
SLAB = SLAB_HEADS * HEAD_DIM


def _split3(x):
    hi = x.astype(BF16)
    r1 = x - hi.astype(F32)
    mid = r1.astype(BF16)
    lo = (r1 - mid.astype(F32)).astype(BF16)
    return hi, mid, lo


def _block_diag(x, block_mask):
    xb = x.astype(BF16)
    return jnp.where(block_mask, jnp.concatenate([xb] * SLAB_HEADS, 0), jnp.zeros((), BF16))


def _per_head_dot(lhs, x, block_mask, nt=False):
    mm = _dot_nt if nt else _dot
    return mm(lhs.astype(BF16), _block_diag(x, block_mask))


def _per_head_sum(x, ones_bd):
    M = x.shape[0]
    s = _dot(jnp.concatenate(_split3(x), 0), ones_bd)
    return s[:M] + s[M:2 * M] + s[2 * M:]


def _wkv_slab_chunk(state, r, k, v, cum, lw, a, kkw, kaw, rkw, gng, gnb, gate, reverse, consts):
    block_mask, ones_bd, strict, incl, eye, lane_head = consts
    C = r.shape[0]
    kk = k * kkw
    k = k * (1.0 + (a - 1.0) * kaw)
    sums = _per_head_sum(jnp.concatenate([kk * kk, r * k * rkw], 0), ones_bd)
    yield
    kk = kk * lax.rsqrt(jnp.maximum(sums[:C], 1e-24))
    bonus = sums[C:]
    dec = jnp.exp(cum)
    inv = jnp.exp(-cum)
    alpha_bar = -kk * jnp.exp(cum - lw)
    beta_t = kk * a * inv
    k_t = k * inv
    r_bar = r * dec

    ar = jnp.concatenate([alpha_bar, r_bar], 0)
    gb = _per_head_dot(ar, beta_t, block_mask, nt=True)
    gk = _per_head_dot(ar, k_t, block_mask, nt=True)
    yield
    a_ab = jnp.where(strict, gb[:C], 0.0)
    a_rb = jnp.where(incl, gb[C:], 0.0)
    a_ak = jnp.where(strict, gk[:C], 0.0)
    a_rk = jnp.where(incl, gk[C:], 0.0)

    x = eye + a_ab
    p = _per_head_dot(a_ab, a_ab, block_mask)
    av = _per_head_dot(jnp.concatenate([a_ak, a_rk], 0), v, block_mask)
    ss = _per_head_dot(ar, state, block_mask, nt=True)
    yield
    m = 2
    while 2 * m < C:
        xp = _per_head_dot(jnp.concatenate([x, p], 0), p, block_mask)
        yield
        x = x + xp[:C]
        p = xp[C:]
        m *= 2
    xp = _per_head_dot(x, p, block_mask)
    yield
    x = x + xp
    u = _per_head_dot(x, ss[:C] + av[:C], block_mask)
    yield
    y = ss[C:] + _per_head_dot(a_rb, u, block_mask) + av[C:]

    total = cum[0:1] if reverse else cum[C - 1:C]
    end = jnp.exp(total)
    cross = lax.dot_general(jnp.concatenate([u, v], 0).astype(BF16),
                            jnp.concatenate([beta_t * end, k_t * end], 0).astype(BF16),
                            TN_DIMS, preferred_element_type=F32)
    mu = _per_head_sum(y, ones_bd) * (1.0 / HEAD_DIM)
    yield
    new_state = state * end
    for hh in range(SLAB_HEADS):
        new_state = new_state + jnp.where(lane_head == hh, cross[hh * HEAD_DIM:(hh + 1) * HEAD_DIM], 0.0)

    yc = y - mu
    var = _per_head_sum(yc * yc, ones_bd) * (1.0 / HEAD_DIM)
    yield
    out = yc * lax.rsqrt(var + GN_EPS) * gng + gnb + bonus * v
    return new_state, out * gate


def _wkv_kernel(rf, kf, vf, lwf, af, gf, rb, kb, vb, lwb, ab, gb,
                kk_ref, ka_ref, rk_ref, gng_ref, gnb_ref, of_ref, ob_ref, state_ref):
    c = pl.program_id(1)
    C = rf.shape[1]
    D = rf.shape[2]
    N = HEAD_DIM

    @pl.when(c == 0)
    def _():
        state_ref[...] = jnp.zeros_like(state_ref)

    row = lax.broadcasted_iota(jnp.int32, (SLAB, SLAB), 0)
    col = lax.broadcasted_iota(jnp.int32, (SLAB, SLAB), 1)
    block_mask = (row // N) == (col // N)
    ones_bd = jnp.where(block_mask, 1.0, 0.0).astype(BF16)
    t = lax.broadcasted_iota(jnp.int32, (C, SLAB), 0)
    lane = lax.broadcasted_iota(jnp.int32, (C, SLAB), 1)
    s = lane % C
    lane_head = lax.broadcasted_iota(jnp.int32, (N, SLAB), 1) // N
    ti = lax.broadcasted_iota(jnp.int32, (C, C), 0)
    si = lax.broadcasted_iota(jnp.int32, (C, C), 1)
    eye = (s == t).astype(F32)

    chains = []
    for d, (refs, o_ref) in enumerate((((rf, kf, vf, lwf, af, gf), of_ref), ((rb, kb, vb, lwb, ab, gb), ob_ref))):
        reverse = d == 1
        strict = (s > t) if reverse else (s < t)
        incl = (s >= t) if reverse else (s <= t)
        consts = (block_mask, ones_bd, strict, incl, eye, lane_head)
        tri = ((si >= ti) if reverse else (si <= ti)).astype(BF16)
        lw_all = refs[3][0]
        parts = _dot(tri, jnp.concatenate(_split3(lw_all), 1))
        cum_all = parts[:, :D] + parts[:, D:2 * D] + parts[:, 2 * D:]
        for sb in range(D // SLAB):
            sl = slice(sb * SLAB, (sb + 1) * SLAB)
            r, k, v, lw, a, gate = (ref[0, :, sl] for ref in refs)
            chains.append((d, sb, sl, o_ref, _wkv_slab_chunk(
                state_ref[d, sb], r, k, v, cum_all[:, sl], lw, a,
                kk_ref[d:d + 1, sl], ka_ref[d:d + 1, sl], rk_ref[:, sl], gng_ref[:, sl], gnb_ref[:, sl],
                gate, reverse, consts)))

    results = _run_pipelined([chain for *_, chain in chains], skew=0)
    for (d, sb, sl, o_ref, _), (new_state, out) in zip(chains, results):
        state_ref[d, sb] = new_state
        o_ref[0, :, sl] = out


def rwkv_scan(fwd, bwd, k_k, k_a, r_k, gn_g, gn_b):
    B, S, D = fwd[0].shape
    C = WKV_CHUNK
    NC = S // C
    f_spec = pl.BlockSpec((1, C, D), lambda b, c: (b, c, 0))
    b_spec = pl.BlockSpec((1, C, D), lambda b, c: (b, NC - 1 - c, 0))
    two = pl.BlockSpec((2, D), lambda b, c: (0, 0))
    one = pl.BlockSpec((1, D), lambda b, c: (0, 0))
    return pl.pallas_call(
        _wkv_kernel,
        grid=(B, NC),
        in_specs=[f_spec] * 6 + [b_spec] * 6 + [two, two, one, one, one],
        out_specs=[f_spec, b_spec],
        out_shape=[jax.ShapeDtypeStruct((B, S, D), F32)] * 2,
        scratch_shapes=[pltpu.VMEM((2, D // SLAB, HEAD_DIM, SLAB), F32)],
        compiler_params=_params("parallel", "arbitrary"),
        name="rwkv_scan",
    )(*fwd, *bwd, k_k, k_a, r_k.reshape(1, D), gn_g.reshape(1, D), gn_b.reshape(1, D))


def _xatt_kernel(x_ref, kv_ref, wq_ref, wo_ref, g_ref, b_ref, o_ref, *, alpha):
    x = x_ref[0]
    D = x.shape[-1]
    E = D // XATT_HEADS
    q = (_dot(x.astype(BF16), wq_ref[...]) * (E ** -0.5)).astype(BF16)
    outs = []
    for h in range(XATT_HEADS):
        kh = kv_ref[0, :, h * E:(h + 1) * E]
        vh = kv_ref[0, :, D + h * E:D + (h + 1) * E]
        s = _dot_nt(q[:, h * E:(h + 1) * E], kh)
        p = jnp.exp(s - jnp.max(s, -1, keepdims=True))
        inv = 1.0 / jnp.sum(p, -1, keepdims=True)
        outs.append((_dot(p.astype(BF16), vh) * inv).astype(BF16))
    y = alpha * x + _dot(jnp.concatenate(outs, -1), wo_ref[...])
    o_ref[0] = _layer_norm(y, g_ref[...], b_ref[...])


def cross_attention_sublayer(x, kv, w_q, w_o, g, b, alpha, tm):
    B, S, D = x.shape
    M = kv.shape[1]
    fixed = lambda bb, i: (0, 0)
    tile = pl.BlockSpec((1, tm, D), lambda bb, i: (bb, i, 0))
    return pl.pallas_call(
        functools.partial(_xatt_kernel, alpha=alpha),
        grid=(B, S // tm),
        in_specs=[tile, pl.BlockSpec((1, M, 2 * D), lambda bb, i: (bb, 0, 0)),
                  pl.BlockSpec((D, D), fixed), pl.BlockSpec((D, D), fixed),
                  pl.BlockSpec((1, D), fixed), pl.BlockSpec((1, D), fixed)],
        out_specs=tile,
        out_shape=jax.ShapeDtypeStruct((B, S, D), F32),
        compiler_params=_params("parallel", "arbitrary"),
        name="cross_attention",
    )(x, kv, w_q, w_o, g.reshape(1, D), b.reshape(1, D))


FFN_LANES = 128


def _ffn_in_kernel(x_ref, prev_ref, next_ref, w_ref, cw_ref, cb_ref, o_ref):
    i = pl.program_id(2)
    n_i = pl.num_programs(2)
    tm = x_ref.shape[1]
    C = FFN_LANES
    x = jnp.concatenate([prev_ref[0], x_ref[0], next_ref[0]], 0).astype(BF16)
    r = lax.broadcasted_iota(jnp.int32, (tm, 1), 0)

    def column_chunk(n):
        y = _dot(x, w_ref[:, 2 * C * n:2 * C * (n + 1)])
        yield
        gate = y[8:8 + tm, :C]
        val = y[8:8 + tm, C:]
        g_prev = jnp.where(i == 0, 0.0, y[7:8, :C])
        g_next = jnp.where(i == n_i - 1, 0.0, y[tm + 8:tm + 9, :C])
        before = jnp.where(r == 0, g_prev, pltpu.roll(gate, 1, 0))
        after = jnp.where(r == tm - 1, g_next, pltpu.roll(gate, tm - 1, 0))
        cols = slice(C * n, C * (n + 1))
        conv = before * cw_ref[0:1, cols] + gate * cw_ref[1:2, cols] + after * cw_ref[2:3, cols] + cb_ref[:, cols]
        gelu = 0.5 * conv * (1.0 + lax.erf(conv * (2.0 ** -0.5)))
        o_ref[0, :, cols] = (gelu * val).astype(o_ref.dtype)

    _run_pipelined([column_chunk(n) for n in range(o_ref.shape[2] // C)], skew=1)


def ffn_in(x, w_il, conv_w, conv_b, tm, tn):
    B, S, D = x.shape
    F = w_il.shape[1] // 2
    nb = tm // 8
    return pl.pallas_call(
        _ffn_in_kernel,
        grid=(F // tn, B, S // tm),
        in_specs=[pl.BlockSpec((1, tm, D), lambda j, b, i: (b, i, 0)),
                  pl.BlockSpec((1, 8, D), lambda j, b, i: (b, jnp.maximum(i * nb - 1, 0), 0)),
                  pl.BlockSpec((1, 8, D), lambda j, b, i: (b, jnp.minimum((i + 1) * nb, S // 8 - 1), 0)),
                  pl.BlockSpec((D, 2 * tn), lambda j, b, i: (0, j)),
                  pl.BlockSpec((3, tn), lambda j, b, i: (0, j)),
                  pl.BlockSpec((1, tn), lambda j, b, i: (0, j))],
        out_specs=pl.BlockSpec((1, tm, tn), lambda j, b, i: (b, i, j)),
        out_shape=jax.ShapeDtypeStruct((B, S, F), BF16),
        compiler_params=_params("arbitrary", "arbitrary", "arbitrary"),
        name="ffn_in",
    )(x, x, x, w_il, conv_w, conv_b.reshape(1, F))


def interleave_gate_value(w_in):
    D, F2 = w_in.shape
    F = F2 // 2
    parts = w_in.reshape(D, 2, F // FFN_LANES, FFN_LANES)
    return parts.transpose(0, 2, 1, 3).reshape(D, F2)


def _alibi_slopes(n):
    return jnp.exp2(-8.0 * jnp.arange(1, n + 1, dtype=F32) / n)


def kernel(x, mem, diff_w_qkv, diff_lambda, diff_subln, diff_w_o, dil_w_qkv, dil_w_o, rwkv_mu, rwkv_w_rkv, rwkv_w0, rwkv_w1, rwkv_w2, rwkv_a0, rwkv_a1, rwkv_a2, rwkv_g1, rwkv_g2, rwkv_k_k, rwkv_k_a, rwkv_r_k, rwkv_gn_g, rwkv_gn_b, rwkv_w_o, xatt_w_q, xatt_w_kv, xatt_w_o, ffn_w_in, ffn_conv_w, ffn_conv_b, ffn_w_out, ln_g, ln_b):
    B, S, D = x.shape
    depth = xatt_w_q.shape[0]
    alpha = (2 * depth) ** 0.25
    T = B * S
    bf = lambda a: a.astype(BF16)
    mem2 = mem.reshape(-1, D)

    for i in range(depth):
        m, j = i % N_MIXERS, i // N_MIXERS
        x2 = x.reshape(T, D)
        if m == 0:
            lambda_init = 0.8 - 0.6 * math.exp(-0.3 * i)
            lamf = diff_lambda[j].astype(F32)
            lam_full = (jnp.exp(jnp.sum(lamf[0] * lamf[1])) - jnp.exp(jnp.sum(lamf[2] * lamf[3]))
                        + lambda_init).reshape(1)
            q_scale = jnp.where(jnp.arange(3 * D) < D, LOG2E * HEAD_DIM ** -0.5, 1.0).astype(F32)
            qkv = matmul(x2, bf(diff_w_qkv[j]), BF16, tm=1024, tn=512, col_scale=q_scale).reshape(B, S, 3 * D)
            o = diff_attention_core(qkv, _split_slopes(LOG2E * _alibi_slopes(D // (2 * HEAD_DIM))), lam_full,
                                    diff_subln[j], lambda_init, tq=512)
            hs, w_o = [o.reshape(T, D)], diff_w_o[j]
        elif m == 1:
            qkv = matmul(x2, bf(dil_w_qkv[j]), F32, tm=1024, tn=512).reshape(B, S, -1)
            slopes = _alibi_slopes(D // HEAD_DIM)
            parts = [dilated_group_core(qkv, slopes, g, window, dilation, tq=128)
                     for g, (window, dilation) in enumerate(DIL_PATTERNS)]
            hs = [dilated_merge([p[0] for p in parts], [p[1] for p in parts], tm=512)]
            w_o = dil_w_o[j]
        else:
            dirs = [rwkv_projections(x, rwkv_mu[j, d], bf(rwkv_w_rkv[j]), rwkv_w0[j, d], bf(rwkv_w1[j, d]),
                                     bf(rwkv_w2[j, d]), rwkv_a0[j, d], bf(rwkv_a1[j, d]), bf(rwkv_a2[j, d]),
                                     bf(rwkv_g1[j, d]), bf(rwkv_g2[j, d]), reverse=d == 1, tm=256)
                    for d in range(2)]
            yf, yb = rwkv_scan(dirs[0], dirs[1], rwkv_k_k[j], rwkv_k_a[j], rwkv_r_k[j],
                               rwkv_gn_g[j], rwkv_gn_b[j])
            hs, w_o = [yf.reshape(T, D), yb.reshape(T, D)], rwkv_w_o[j]
        x2 = matmul_residual_ln(hs, bf(w_o), x2, ln_g[i, 0], ln_b[i, 0], alpha, tm=512)

        kv = matmul(mem2, bf(xatt_w_kv[i]), BF16, tm=512, tn=512).reshape(B, -1, 2 * D)
        x = cross_attention_sublayer(x2.reshape(B, S, D), kv, bf(xatt_w_q[i]), bf(xatt_w_o[i]),
                                     ln_g[i, 1], ln_b[i, 1], alpha, tm=512)

        act = ffn_in(x, interleave_gate_value(bf(ffn_w_in[i])), ffn_conv_w[i], ffn_conv_b[i], tm=512, tn=1408)
        x = matmul_residual_ln([act.reshape(T, -1)], bf(ffn_w_out[i]), x.reshape(T, D),
                               ln_g[i, 2], ln_b[i, 2], alpha, tm=512).reshape(B, S, D)
    return x
```

```python
import functools
import math

import jax
import jax.numpy as jnp
from jax import lax
from jax.experimental import pallas as pl
from jax.experimental.pallas import tpu as pltpu

BF16 = jnp.bfloat16
F32 = jnp.float32

HEAD_DIM = 64
N_MIXERS = 3
DIL_PATTERNS = ((128, 1), (512, 4), (2048, 16))
XATT_HEADS = 4
LN_EPS = 1e-5
GN_EPS = 64e-5
NEG_INF = -1e30
WKV_CHUNK = 64
VMEM_LIMIT_BYTES = 56 * 1024 * 1024
LOG2E = math.log2(math.e)

NT_DIMS = (((1,), (1,)), ((), ()))
TN_DIMS = (((0,), (0,)), ((), ()))


def _params(*semantics):
    return pltpu.CompilerParams(dimension_semantics=semantics, vmem_limit_bytes=VMEM_LIMIT_BYTES)


def _dot(a, b):
    return jnp.dot(a, b, preferred_element_type=F32)


def _dot_nt(a, b, precision=None):
    return lax.dot_general(a, b, NT_DIMS, preferred_element_type=F32, precision=precision)


def _layer_norm(y, g, b):
    mu = jnp.mean(y, -1, keepdims=True)
    yc = y - mu
    var = jnp.mean(yc * yc, -1, keepdims=True)
    return yc * lax.rsqrt(var + LN_EPS) * g + b


def _run_pipelined(chains, skew):
    results = [None] * len(chains)
    live = [True] * len(chains)
    tick = 0
    while any(live):
        for n, chain in enumerate(chains):
            if live[n] and tick >= n * skew:
                try:
                    next(chain)
                except StopIteration as stop:
                    results[n] = stop.value
                    live[n] = False
        tick += 1
    return results


MM_COL_CHUNK = 512


def _mm_kernel(x_ref, w_ref, *rest):
    o_ref = rest[-1]
    x = x_ref[...].astype(BF16)
    for c0 in range(0, o_ref.shape[1], MM_COL_CHUNK):
        cols = slice(c0, c0 + MM_COL_CHUNK)
        y = _dot(x, w_ref[:, cols])
        if len(rest) == 2:
            y = y * rest[0][:, cols]
        o_ref[:, cols] = y.astype(o_ref.dtype)


def matmul(x, w, out_dtype, tm, tn, col_scale=None):
    M, K = x.shape
    N = w.shape[1]
    in_specs = [pl.BlockSpec((tm, K), lambda j, i: (i, 0)), pl.BlockSpec((K, tn), lambda j, i: (0, j))]
    operands = [x, w]
    if col_scale is not None:
        in_specs.append(pl.BlockSpec((1, tn), lambda j, i: (0, j)))
        operands.append(col_scale.reshape(1, N))
    return pl.pallas_call(
        _mm_kernel,
        grid=(N // tn, M // tm),
        in_specs=in_specs,
        out_specs=pl.BlockSpec((tm, tn), lambda j, i: (i, j)),
        out_shape=jax.ShapeDtypeStruct((M, N), out_dtype),
        compiler_params=_params("arbitrary", "arbitrary"),
        name="matmul",
    )(*operands)


ROW_CHUNK = 256


def _mm_res_ln_kernel(*refs, n_h, alpha):
    h_refs = refs[:n_h]
    w_ref, x_ref, g_ref, b_ref, o_ref = refs[n_h:]

    def row_chunk(r0):
        rows = slice(r0, r0 + ROW_CHUNK)
        h = h_refs[0][rows, :]
        for r in h_refs[1:]:
            h = h + r[rows, :]
        hw = _dot(h.astype(BF16), w_ref[...])
        yield
        o_ref[rows, :] = _layer_norm(alpha * x_ref[rows, :] + hw, g_ref[...], b_ref[...])

    _run_pipelined([row_chunk(r0) for r0 in range(0, o_ref.shape[0], ROW_CHUNK)], skew=1)


def matmul_residual_ln(hs, w, x, g, b, alpha, tm):
    M, K = hs[0].shape
    D = w.shape[1]
    row = lambda i: (i, 0)
    fixed = lambda i: (0, 0)
    return pl.pallas_call(
        functools.partial(_mm_res_ln_kernel, n_h=len(hs), alpha=alpha),
        grid=(M // tm,),
        in_specs=[pl.BlockSpec((tm, K), row) for _ in hs]
        + [pl.BlockSpec((K, D), fixed), pl.BlockSpec((tm, D), row),
           pl.BlockSpec((1, D), fixed), pl.BlockSpec((1, D), fixed)],
        out_specs=pl.BlockSpec((tm, D), row),
        out_shape=jax.ShapeDtypeStruct((M, D), F32),
        compiler_params=_params("parallel"),
        name="matmul_residual_ln",
    )(*hs, w, x, g.reshape(1, D), b.reshape(1, D))


ALIBI_SPLIT = 3
ALIBI_FEATS = 4 * ALIBI_SPLIT


def _alibi_features(pos, lane0, cs, key_side):
    lane = lax.broadcasted_iota(jnp.int32, (1, 2 * HEAD_DIM), 1) - lane0
    kind = lane & 3
    piece = lane >> 2
    live = (lane >= 0) & (lane < ALIBI_FEATS)
    c = jnp.where(piece == 0, cs[0], jnp.where(piece == 1, cs[1], cs[2]))
    hi = (pos >> 6).astype(F32)
    lo = (pos & 63).astype(F32)
    if key_side:
        f = jnp.where(kind == 0, -64.0 * c, jnp.where(kind == 1, -c, jnp.where(kind == 2, 64.0 * hi, lo)))
    else:
        f = jnp.where(kind == 0, hi, jnp.where(kind == 1, lo, c))
    return jnp.where(live, f, 0.0)


def _diff_attn_kernel(cs_ref, lam_ref, q_ref, k_ref, v_ref, subln_ref, o_ref, kaug_ref, vaug_ref, s_ref,
                      *, tq, out_scale):
    h = pl.program_id(1)
    qi = pl.program_id(2)
    E = HEAD_DIM
    S = k_ref.shape[1]
    nb = S // tq
    cs = [cs_ref[ALIBI_SPLIT * h + m] for m in range(ALIBI_SPLIT)]
    lane = lax.broadcasted_iota(jnp.int32, (1, 2 * E), 1)
    first = lane < E

    @pl.when(qi == 0)
    def _():
        k = k_ref[0]
        pos = lax.broadcasted_iota(jnp.int32, (S, 1), 0)
        for c, lane0 in enumerate((E, 0)):
            feat = _alibi_features(pos, lane0, cs, key_side=True)
            own = first if c == 0 else jnp.logical_not(first)
            kaug_ref[c, 0] = jnp.where(own, k, feat.astype(BF16))
            kaug_ref[c, 1] = jnp.where(own, k, (-feat).astype(BF16))
        ones = jnp.where(lane == 0, 1.0, 0.0).astype(BF16)
        vaug_ref[...] = jnp.concatenate([v_ref[0], jnp.broadcast_to(ones, (S, 2 * E))], 1)

    start = pl.multiple_of(qi * tq, tq)
    half = tq // 2

    def block_max(t):
        bm = t[:, :2 * E]
        for g in range(1, tq // (2 * E)):
            bm = jnp.maximum(bm, t[:, g * 2 * E:(g + 1) * 2 * E])
        return bm

    def softmax_map(c, r0):
        rows = slice(r0, r0 + half)
        own = first if c == 0 else jnp.logical_not(first)
        pos_q = qi * tq + r0 + lax.broadcasted_iota(jnp.int32, (half, 1), 0)
        q = q_ref[0, rows, :]
        qa = jnp.where(own, q, _alibi_features(pos_q, E if c == 0 else 0, cs, key_side=False).astype(BF16))
        lane_max = jnp.full((half, 2 * E), NEG_INF, F32)
        for jb in range(nb):
            side = (jb > qi).astype(jnp.int32)
            t = _dot_nt(qa, kaug_ref[c, side, jb * tq:(jb + 1) * tq, :])
            s_ref[c, jb, rows, :] = t
            lane_max = jnp.maximum(lane_max, jnp.where(jb == qi, NEG_INF, block_max(t)))
            yield
        t = jnp.minimum(s_ref[c, qi, rows, :], _dot_nt(qa, kaug_ref[c, 1, pl.ds(start, tq), :]))
        s_ref[c, qi, rows, :] = t
        m = jnp.max(jnp.maximum(lane_max, block_max(t)), -1, keepdims=True)
        yield
        ps = []
        for jb in range(nb):
            ps.append(jnp.exp2(s_ref[c, jb, rows, :] - m).astype(BF16))
            yield
        ol = jnp.zeros((half, 4 * E), F32)
        for jb in range(nb):
            ol = ol + _dot(ps[jb], vaug_ref[jb * tq:(jb + 1) * tq, :])
            yield
        return ol[:, :2 * E] * (1.0 / ol[:, 2 * E:2 * E + 1])

    items = [(c, r0) for r0 in (0, half) for c in (0, 1)]
    results = _run_pipelined([softmax_map(c, r0) for c, r0 in items], skew=nb + 1)
    for n, r0 in enumerate((0, half)):
        o = results[2 * n] - lam_ref[0] * results[2 * n + 1]
        o = o * lax.rsqrt(jnp.mean(o * o, -1, keepdims=True) + LN_EPS) * subln_ref[...] * out_scale
        o_ref[0, r0:r0 + half, :] = o.astype(o_ref.dtype)


def diff_attention_core(qkv, slope_pieces, lam_full, subln, lambda_init, tq):
    B, S, D3 = qkv.shape
    D = D3 // 3
    H = D // (2 * HEAD_DIM)
    W = 2 * HEAD_DIM
    smem = pl.BlockSpec(memory_space=pltpu.SMEM)
    return pl.pallas_call(
        functools.partial(_diff_attn_kernel, tq=tq, out_scale=1.0 - lambda_init),
        grid=(B, H, S // tq),
        in_specs=[smem, smem,
                  pl.BlockSpec((1, tq, W), lambda b, h, i: (b, i, h)),
                  pl.BlockSpec((1, S, W), lambda b, h, i: (b, 0, H + h)),
                  pl.BlockSpec((1, S, W), lambda b, h, i: (b, 0, 2 * H + h)),
                  pl.BlockSpec((1, W), lambda b, h, i: (0, 0))],
        out_specs=pl.BlockSpec((1, tq, W), lambda b, h, i: (b, i, h)),
        out_shape=jax.ShapeDtypeStruct((B, S, D), BF16),
        scratch_shapes=[pltpu.VMEM((2, 2, S, W), BF16), pltpu.VMEM((S, 2 * W), BF16),
                        pltpu.VMEM((2, S // tq, tq, tq), F32)],
        compiler_params=_params("parallel", "parallel", "arbitrary"),
        name="diff_attention",
    )(slope_pieces, lam_full, qkv, qkv, qkv, subln.reshape(1, W))


def _split_slopes(slopes):
    pieces, rest = [], slopes
    for _ in range(ALIBI_SPLIT):
        piece = rest.astype(BF16).astype(F32)
        pieces.append(piece)
        rest = rest - piece
    return jnp.stack(pieces, -1).reshape(-1)


DIL_STEP_TOKENS = 2048


def _dil_attn_kernel(slope_ref, q_ref, k_ref, v_ref, o_ref, lse_ref, *, tq, half, dilation, nq):
    hp = pl.program_id(1)
    blk = pl.program_id(2)
    E = HEAD_DIM
    L = k_ref.shape[1] // dilation
    kw = tq + 2 * half
    lane = lax.broadcasted_iota(jnp.int32, (tq, 2 * E), 1)
    first = lane < E
    zero = jnp.zeros((tq, 2 * E), BF16)
    slope0 = slope_ref[2 * hp]
    slope1 = slope_ref[2 * hp + 1]
    row = lax.broadcasted_iota(jnp.int32, (tq, 1), 0)
    colk = lax.broadcasted_iota(jnp.int32, (1, kw), 1)

    def rows(start, size):
        return pl.ds(start, size) if dilation == 1 else pl.ds(start, size, stride=dilation)

    for j in range(nq):
        uq = (blk * nq + j) * tq
        ks = jnp.clip(uq - half, 0, L - kw)
        rel = jnp.abs((ks + colk) - (uq + row))
        valid = rel <= half
        dist = (dilation * rel).astype(F32)
        for r in range(dilation):
            local = rows(j * tq * dilation + r, tq)
            window = rows(ks * dilation + r, kw)
            q = (q_ref[0, local, :] * (E ** -0.5)).astype(BF16)
            k = k_ref[0, window, :].astype(BF16)
            v = v_ref[0, window, :].astype(BF16)

            def one_head(qc, slope):
                s = jnp.where(valid, _dot_nt(qc, k) - slope * dist, NEG_INF)
                m = jnp.max(s, -1, keepdims=True)
                p = jnp.exp(s - m)
                l = jnp.sum(p, -1, keepdims=True)
                o = _dot(p.astype(BF16), v) * (1.0 / l)
                return o, m + jnp.log(l)

            o0, lse0 = one_head(jnp.where(first, q, zero), slope0)
            o1, lse1 = one_head(jnp.where(first, zero, q), slope1)
            o_ref[0, local, :] = jnp.where(first, o0, o1)
            lse_ref[0, local, :] = jnp.where(first, lse0, lse1)


def dilated_group_core(qkv, slopes, g, window, dilation, tq):
    B, S, C = qkv.shape
    G = len(DIL_PATTERNS)
    D = C // (3 * G)
    W = 2 * HEAD_DIM
    HP = D // W
    half = window // (2 * dilation)
    base = g * 3 * HP
    nq = DIL_STEP_TOKENS // (tq * dilation)
    smem = pl.BlockSpec(memory_space=pltpu.SMEM)
    out_spec = pl.BlockSpec((1, DIL_STEP_TOKENS, W), lambda b, hp, i: (b, i, hp))
    o, lse = pl.pallas_call(
        functools.partial(_dil_attn_kernel, tq=tq, half=half, dilation=dilation, nq=nq),
        grid=(B, HP, S // DIL_STEP_TOKENS),
        in_specs=[smem,
                  pl.BlockSpec((1, DIL_STEP_TOKENS, W), lambda b, hp, i: (b, i, base + hp)),
                  pl.BlockSpec((1, S, W), lambda b, hp, i: (b, 0, base + HP + hp)),
                  pl.BlockSpec((1, S, W), lambda b, hp, i: (b, 0, base + 2 * HP + hp))],
        out_specs=[out_spec, out_spec],
        out_shape=[jax.ShapeDtypeStruct((B, S, D), F32)] * 2,
        compiler_params=_params("parallel", "parallel", "arbitrary"),
        name=f"dilated_attention_d{dilation}",
    )(slopes, qkv, qkv, qkv)
    return o.reshape(B * S, D), lse.reshape(B * S, D)


def _dil_merge_kernel(o0, o1, o2, l0, l1, l2, out_ref):
    a, b, c = l0[...], l1[...], l2[...]
    m = jnp.maximum(jnp.maximum(a, b), c)
    ea, eb, ec = jnp.exp(a - m), jnp.exp(b - m), jnp.exp(c - m)
    inv = 1.0 / (ea + eb + ec)
    out_ref[...] = ((ea * inv) * o0[...] + (eb * inv) * o1[...] + (ec * inv) * o2[...]).astype(out_ref.dtype)


def dilated_merge(outs, lses, tm):
    M, D = outs[0].shape
    spec = pl.BlockSpec((tm, D), lambda i: (i, 0))
    return pl.pallas_call(
        _dil_merge_kernel,
        grid=(M // tm,),
        in_specs=[spec] * 6,
        out_specs=spec,
        out_shape=jax.ShapeDtypeStruct((M, D), BF16),
        compiler_params=_params("parallel"),
        name="dilated_merge",
    )(*outs, *lses)


def _shifted_rows(x, prev_ref, next_ref, i, n_i, reverse):
    tm = x.shape[0]
    r = lax.broadcasted_iota(jnp.int32, (tm, 1), 0)
    if reverse:
        edge = jnp.where(i == n_i - 1, 0.0, next_ref[0, 0:1, :])
        return jnp.where(r == tm - 1, edge, pltpu.roll(x, tm - 1, 0))
    edge = jnp.where(i == 0, 0.0, prev_ref[0, 7:8, :])
    return jnp.where(r == 0, edge, pltpu.roll(x, 1, 0))


def _rwkv_proj_kernel(x_ref, prev_ref, next_ref, mu_ref, wr_ref, wk_ref, wv_ref, w0_ref, w1_ref, w2_ref,
                      a0_ref, a1_ref, a2_ref, g1_ref, g2_ref,
                      r_ref, k_ref, v_ref, lw_ref, a_ref, g_ref, *, reverse):
    i = pl.program_id(1)
    x = x_ref[0]
    xx = _shifted_rows(x, prev_ref, next_ref, i, pl.num_programs(1), reverse) - x
    mu = mu_ref[...]

    def mix(j):
        return (x + xx * mu[j:j + 1]).astype(BF16)

    r_ref[0] = _dot(mix(0), wr_ref[...]).astype(r_ref.dtype)
    k_ref[0] = _dot(mix(2), wk_ref[...]).astype(k_ref.dtype)
    v_ref[0] = _dot(mix(3), wv_ref[...]).astype(v_ref.dtype)
    z = -(w0_ref[...] + _dot(jnp.tanh(_dot(mix(1), w1_ref[...])).astype(BF16), w2_ref[...]))
    softplus = jnp.maximum(z, 0.0) + jnp.log(1.0 + jnp.exp(-jnp.abs(z)))
    lw_ref[0] = -jnp.exp(-softplus - 0.5)
    a_ref[0] = jax.nn.sigmoid(a0_ref[...] + _dot(_dot(mix(4), a1_ref[...]).astype(BF16), a2_ref[...]))
    g_ref[0] = _dot(jax.nn.sigmoid(_dot(mix(5), g1_ref[...])).astype(BF16), g2_ref[...]).astype(g_ref.dtype)


def rwkv_projections(x, mu, w_rkv, w0, w1, w2, a0, a1, a2, g1, g2, reverse, tm):
    B, S, D = x.shape
    nb = tm // 8
    fixed2 = lambda b, i: (0, 0)
    tile = pl.BlockSpec((1, tm, D), lambda b, i: (b, i, 0))
    full = lambda a: pl.BlockSpec(a.shape, fixed2)
    ops = (mu, w_rkv[0], w_rkv[1], w_rkv[2], w0.reshape(1, D), w1, w2, a0.reshape(1, D), a1, a2, g1, g2)
    return pl.pallas_call(
        functools.partial(_rwkv_proj_kernel, reverse=reverse),
        grid=(B, S // tm),
        in_specs=[tile,
                  pl.BlockSpec((1, 8, D), lambda b, i: (b, jnp.maximum(i * nb - 1, 0), 0)),
                  pl.BlockSpec((1, 8, D), lambda b, i: (b, jnp.minimum((i + 1) * nb, S // 8 - 1), 0))]
        + [full(a) for a in ops],
        out_specs=[tile] * 6,
        out_shape=[jax.ShapeDtypeStruct((B, S, D), dt) for dt in (BF16, BF16, BF16, F32, F32, BF16)],
        compiler_params=_params("parallel", "arbitrary"),
        name="rwkv_projections_bwd" if reverse else "rwkv_projections_fwd",
    )(x, x, x, *ops)


SLAB_HEADS = 4
SLAB = SLAB_HEADS * HEAD_DIM


def _split2(x):
    hi = x.astype(BF16)
    lo = (x - hi.astype(F32)).astype(BF16)
    return hi, lo


def _block_diag(x, block_mask):
    xb = x.astype(BF16)
    return jnp.where(block_mask, jnp.concatenate([xb] * SLAB_HEADS, 0), jnp.zeros((), BF16))


def _per_head_dot(lhs, x, block_mask, nt=False):
    mm = _dot_nt if nt else _dot
    return mm(lhs.astype(BF16), _block_diag(x, block_mask))


def _per_head_sum(x, ones_bd):
    M = x.shape[0]
    s = _dot(jnp.concatenate(_split2(x), 0), ones_bd)
    return s[:M] + s[M:]


def _wkv_slab_chunk(state, r, k, v, cum, lw, a, kkw, kaw, rkw, gng, gnb, gate, reverse, consts):
    block_mask, ones_bd, strict, incl, eye, lane_head = consts
    C = r.shape[0]
    kk = k * kkw
    k = k * (1.0 + (a - 1.0) * kaw)
    sums = _per_head_sum(jnp.concatenate([kk * kk, r * k * rkw], 0), ones_bd)
    yield
    kk = kk * lax.rsqrt(jnp.maximum(sums[:C], 1e-24))
    bonus = sums[C:]
    dec = jnp.exp(cum)
    inv = jnp.exp(-cum)
    alpha_bar = -kk * jnp.exp(cum - lw)
    beta_t = kk * a * inv
    k_t = k * inv
    r_bar = r * dec

    ar = jnp.concatenate([alpha_bar, r_bar], 0)
    gb = _per_head_dot(ar, beta_t, block_mask, nt=True)
    gk = _per_head_dot(ar, k_t, block_mask, nt=True)
    yield
    a_ab = jnp.where(strict, gb[:C], 0.0)
    a_rb = jnp.where(incl, gb[C:], 0.0)
    a_ak = jnp.where(strict, gk[:C], 0.0)
    a_rk = jnp.where(incl, gk[C:], 0.0)

    x = eye + a_ab
    p = _per_head_dot(a_ab, a_ab, block_mask)
    av = _per_head_dot(jnp.concatenate([a_ak, a_rk], 0), v, block_mask)
    ss = _per_head_dot(ar, state, block_mask, nt=True)
    yield
    m = 2
    while 2 * m < C:
        xp = _per_head_dot(jnp.concatenate([x, p], 0), p, block_mask)
        yield
        x = x + xp[:C]
        p = xp[C:]
        m *= 2
    xp = _per_head_dot(x, p, block_mask)
    yield
    x = x + xp
    u = _per_head_dot(x, ss[:C] + av[:C], block_mask)
    yield
    y = ss[C:] + _per_head_dot(a_rb, u, block_mask) + av[C:]

    total = cum[0:1] if reverse else cum[C - 1:C]
    end = jnp.exp(total)
    cross = lax.dot_general(jnp.concatenate([u, v], 0).astype(BF16),
                            jnp.concatenate([beta_t * end, k_t * end], 0).astype(BF16),
                            TN_DIMS, preferred_element_type=F32)
    mu = _per_head_sum(y, ones_bd) * (1.0 / HEAD_DIM)
    yield
    new_state = state * end
    for hh in range(SLAB_HEADS):
        new_state = new_state + jnp.where(lane_head == hh, cross[hh * HEAD_DIM:(hh + 1) * HEAD_DIM], 0.0)

    yc = y - mu
    var = _per_head_sum(yc * yc, ones_bd) * (1.0 / HEAD_DIM)
    yield
    out = yc * lax.rsqrt(var + GN_EPS) * gng + gnb + bonus * v
    return new_state, out * gate


def _wkv_kernel(rf, kf, vf, lwf, af, gf, rb, kb, vb, lwb, ab, gb,
                kk_ref, ka_ref, rk_ref, gng_ref, gnb_ref, of_ref, ob_ref, state_ref):
    c = pl.program_id(1)
    C = rf.shape[1]
    D = rf.shape[2]
    N = HEAD_DIM

    @pl.when(c == 0)
    def _():
        state_ref[...] = jnp.zeros_like(state_ref)

    row = lax.broadcasted_iota(jnp.int32, (SLAB, SLAB), 0)
    col = lax.broadcasted_iota(jnp.int32, (SLAB, SLAB), 1)
    block_mask = (row // N) == (col // N)
    ones_bd = jnp.where(block_mask, 1.0, 0.0).astype(BF16)
    t = lax.broadcasted_iota(jnp.int32, (C, SLAB), 0)
    lane = lax.broadcasted_iota(jnp.int32, (C, SLAB), 1)
    s = lane % C
    lane_head = lax.broadcasted_iota(jnp.int32, (N, SLAB), 1) // N
    ti = lax.broadcasted_iota(jnp.int32, (C, C), 0)
    si = lax.broadcasted_iota(jnp.int32, (C, C), 1)
    eye = (s == t).astype(F32)

    chains = []
    for d, (refs, o_ref) in enumerate((((rf, kf, vf, lwf, af, gf), of_ref), ((rb, kb, vb, lwb, ab, gb), ob_ref))):
        reverse = d == 1
        strict = (s > t) if reverse else (s < t)
        incl = (s >= t) if reverse else (s <= t)
        consts = (block_mask, ones_bd, strict, incl, eye, lane_head)
        tri = ((si >= ti) if reverse else (si <= ti)).astype(BF16)
        lw_all = refs[3][0]
        parts = _dot(tri, jnp.concatenate(_split2(lw_all), 1))
        cum_all = parts[:, :D] + parts[:, D:]
        for sb in range(D // SLAB):
            sl = slice(sb * SLAB, (sb + 1) * SLAB)
            r, k, v, lw, a, gate = (ref[0, :, sl].astype(F32) for ref in refs)
            chains.append((d, sb, sl, o_ref, _wkv_slab_chunk(
                state_ref[d, sb], r, k, v, cum_all[:, sl], lw, a,
                kk_ref[d:d + 1, sl], ka_ref[d:d + 1, sl], rk_ref[:, sl], gng_ref[:, sl], gnb_ref[:, sl],
                gate, reverse, consts)))

    results = _run_pipelined([chain for *_, chain in chains], skew=0)
    for (d, sb, sl, o_ref, _), (new_state, out) in zip(chains, results):
        state_ref[d, sb] = new_state
        o_ref[0, :, sl] = out


def rwkv_scan(fwd, bwd, k_k, k_a, r_k, gn_g, gn_b):
    B, S, D = fwd[0].shape
    C = WKV_CHUNK
    NC = S // C
    f_spec = pl.BlockSpec((1, C, D), lambda b, c: (b, c, 0))
    b_spec = pl.BlockSpec((1, C, D), lambda b, c: (b, NC - 1 - c, 0))
    two = pl.BlockSpec((2, D), lambda b, c: (0, 0))
    one = pl.BlockSpec((1, D), lambda b, c: (0, 0))
    return pl.pallas_call(
        _wkv_kernel,
        grid=(B, NC),
        in_specs=[f_spec] * 6 + [b_spec] * 6 + [two, two, one, one, one],
        out_specs=[f_spec, b_spec],
        out_shape=[jax.ShapeDtypeStruct((B, S, D), F32)] * 2,
        scratch_shapes=[pltpu.VMEM((2, D // SLAB, HEAD_DIM, SLAB), F32)],
        compiler_params=_params("parallel", "arbitrary"),
        name="rwkv_scan",
    )(*fwd, *bwd, k_k, k_a, r_k.reshape(1, D), gn_g.reshape(1, D), gn_b.reshape(1, D))


def _xatt_kernel(x_ref, kv_ref, wq_ref, wo_ref, g_ref, b_ref, o_ref, *, alpha):
    x = x_ref[0]
    D = x.shape[-1]
    E = D // XATT_HEADS
    q = (_dot(x.astype(BF16), wq_ref[...]) * (E ** -0.5)).astype(BF16)
    outs = []
    for h in range(XATT_HEADS):
        kh = kv_ref[0, :, h * E:(h + 1) * E]
        vh = kv_ref[0, :, D + h * E:D + (h + 1) * E]
        s = _dot_nt(q[:, h * E:(h + 1) * E], kh)
        p = jnp.exp(s - jnp.max(s, -1, keepdims=True))
        inv = 1.0 / jnp.sum(p, -1, keepdims=True)
        outs.append((_dot(p.astype(BF16), vh) * inv).astype(BF16))
    y = alpha * x + _dot(jnp.concatenate(outs, -1), wo_ref[...])
    o_ref[0] = _layer_norm(y, g_ref[...], b_ref[...])


def cross_attention_sublayer(x, kv, w_q, w_o, g, b, alpha, tm):
    B, S, D = x.shape
    M = kv.shape[1]
    fixed = lambda bb, i: (0, 0)
    tile = pl.BlockSpec((1, tm, D), lambda bb, i: (bb, i, 0))
    return pl.pallas_call(
        functools.partial(_xatt_kernel, alpha=alpha),
        grid=(B, S // tm),
        in_specs=[tile, pl.BlockSpec((1, M, 2 * D), lambda bb, i: (bb, 0, 0)),
                  pl.BlockSpec((D, D), fixed), pl.BlockSpec((D, D), fixed),
                  pl.BlockSpec((1, D), fixed), pl.BlockSpec((1, D), fixed)],
        out_specs=tile,
        out_shape=jax.ShapeDtypeStruct((B, S, D), F32),
        compiler_params=_params("parallel", "arbitrary"),
        name="cross_attention",
    )(x, kv, w_q, w_o, g.reshape(1, D), b.reshape(1, D))


FFN_LANES = 128


def _ffn_in_kernel(x_ref, prev_ref, next_ref, wg_ref, wv_ref, cw_ref, cb_ref, o_ref, w_ref):
    i = pl.program_id(2)
    n_i = pl.num_programs(2)
    tm = x_ref.shape[1]
    C = FFN_LANES
    n_chunks = o_ref.shape[2] // C

    @pl.when((pl.program_id(1) == 0) & (i == 0))
    def _():
        for n in range(n_chunks):
            w_ref[:, 2 * C * n:2 * C * n + C] = wg_ref[:, C * n:C * (n + 1)]
            w_ref[:, 2 * C * n + C:2 * C * (n + 1)] = wv_ref[:, C * n:C * (n + 1)]

    x = jnp.concatenate([prev_ref[0], x_ref[0], next_ref[0]], 0).astype(BF16)
    r = lax.broadcasted_iota(jnp.int32, (tm, 1), 0)

    def column_chunk(n):
        y = _dot(x, w_ref[:, 2 * C * n:2 * C * (n + 1)])
        yield
        gate = y[8:8 + tm, :C]
        val = y[8:8 + tm, C:]
        g_prev = jnp.where(i == 0, 0.0, y[7:8, :C])
        g_next = jnp.where(i == n_i - 1, 0.0, y[tm + 8:tm + 9, :C])
        before = jnp.where(r == 0, g_prev, pltpu.roll(gate, 1, 0))
        after = jnp.where(r == tm - 1, g_next, pltpu.roll(gate, tm - 1, 0))
        cols = slice(C * n, C * (n + 1))
        conv = before * cw_ref[0:1, cols] + gate * cw_ref[1:2, cols] + after * cw_ref[2:3, cols] + cb_ref[:, cols]
        gelu = 0.5 * conv * (1.0 + lax.erf(conv * (2.0 ** -0.5)))
        o_ref[0, :, cols] = (gelu * val).astype(o_ref.dtype)

    _run_pipelined([column_chunk(n) for n in range(n_chunks)], skew=1)


def ffn_in(x, w_in, conv_w, conv_b, tm, tn):
    B, S, D = x.shape
    F = w_in.shape[1] // 2
    nb = tm // 8
    nj = F // tn
    return pl.pallas_call(
        _ffn_in_kernel,
        grid=(nj, B, S // tm),
        in_specs=[pl.BlockSpec((1, tm, D), lambda j, b, i: (b, i, 0)),
                  pl.BlockSpec((1, 8, D), lambda j, b, i: (b, jnp.maximum(i * nb - 1, 0), 0)),
                  pl.BlockSpec((1, 8, D), lambda j, b, i: (b, jnp.minimum((i + 1) * nb, S // 8 - 1), 0)),
                  pl.BlockSpec((D, tn), lambda j, b, i: (0, j)),
                  pl.BlockSpec((D, tn), lambda j, b, i: (0, nj + j)),
                  pl.BlockSpec((3, tn), lambda j, b, i: (0, j)),
                  pl.BlockSpec((1, tn), lambda j, b, i: (0, j))],
        out_specs=pl.BlockSpec((1, tm, tn), lambda j, b, i: (b, i, j)),
        out_shape=jax.ShapeDtypeStruct((B, S, F), BF16),
        scratch_shapes=[pltpu.VMEM((D, 2 * tn), BF16)],
        compiler_params=_params("arbitrary", "arbitrary", "arbitrary"),
        name="ffn_in",
    )(x, x, x, w_in, w_in, conv_w, conv_b.reshape(1, F))


def _alibi_slopes(n):
    return jnp.exp2(-8.0 * jnp.arange(1, n + 1, dtype=F32) / n)


def kernel(x, mem, diff_w_qkv, diff_lambda, diff_subln, diff_w_o, dil_w_qkv, dil_w_o, rwkv_mu, rwkv_w_rkv, rwkv_w0, rwkv_w1, rwkv_w2, rwkv_a0, rwkv_a1, rwkv_a2, rwkv_g1, rwkv_g2, rwkv_k_k, rwkv_k_a, rwkv_r_k, rwkv_gn_g, rwkv_gn_b, rwkv_w_o, xatt_w_q, xatt_w_kv, xatt_w_o, ffn_w_in, ffn_conv_w, ffn_conv_b, ffn_w_out, ln_g, ln_b):
    B, S, D = x.shape
    depth = xatt_w_q.shape[0]
    alpha = (2 * depth) ** 0.25
    T = B * S
    bf = lambda a: a.astype(BF16)
    mem2 = mem.reshape(-1, D)

    for i in range(depth):
        m, j = i % N_MIXERS, i // N_MIXERS
        x2 = x.reshape(T, D)
        if m == 0:
            lambda_init = 0.8 - 0.6 * math.exp(-0.3 * i)
            lamf = diff_lambda[j].astype(F32)
            lam_full = (jnp.exp(jnp.sum(lamf[0] * lamf[1])) - jnp.exp(jnp.sum(lamf[2] * lamf[3]))
                        + lambda_init).reshape(1)
            q_scale = jnp.where(jnp.arange(3 * D) < D, LOG2E * HEAD_DIM ** -0.5, 1.0).astype(F32)
            qkv = matmul(x2, bf(diff_w_qkv[j]), BF16, tm=1024, tn=3 * D, col_scale=q_scale).reshape(B, S, 3 * D)
            o = diff_attention_core(qkv, _split_slopes(LOG2E * _alibi_slopes(D // (2 * HEAD_DIM))), lam_full,
                                    diff_subln[j], lambda_init, tq=512)
            hs, w_o = [o.reshape(T, D)], diff_w_o[j]
        elif m == 1:
            qkv = matmul(x2, bf(dil_w_qkv[j]), F32, tm=512, tn=3 * D).reshape(B, S, -1)
            slopes = _alibi_slopes(D // HEAD_DIM)
            parts = [dilated_group_core(qkv, slopes, g, window, dilation, tq=128)
                     for g, (window, dilation) in enumerate(DIL_PATTERNS)]
            hs = [dilated_merge([p[0] for p in parts], [p[1] for p in parts], tm=512)]
            w_o = dil_w_o[j]
        else:
            dirs = [rwkv_projections(x, rwkv_mu[j, d], bf(rwkv_w_rkv[j]), rwkv_w0[j, d], bf(rwkv_w1[j, d]),
                                     bf(rwkv_w2[j, d]), rwkv_a0[j, d], bf(rwkv_a1[j, d]), bf(rwkv_a2[j, d]),
                                     bf(rwkv_g1[j, d]), bf(rwkv_g2[j, d]), reverse=d == 1, tm=256)
                    for d in range(2)]
            yf, yb = rwkv_scan(dirs[0], dirs[1], rwkv_k_k[j], rwkv_k_a[j], rwkv_r_k[j],
                               rwkv_gn_g[j], rwkv_gn_b[j])
            hs, w_o = [yf.reshape(T, D), yb.reshape(T, D)], rwkv_w_o[j]
        x2 = matmul_residual_ln(hs, bf(w_o), x2, ln_g[i, 0], ln_b[i, 0], alpha, tm=512)

        kv = matmul(mem2, bf(xatt_w_kv[i]), BF16, tm=512, tn=2 * D).reshape(B, -1, 2 * D)
        x = cross_attention_sublayer(x2.reshape(B, S, D), kv, bf(xatt_w_q[i]), bf(xatt_w_o[i]),
                                     ln_g[i, 1], ln_b[i, 1], alpha, tm=512)

        act = ffn_in(x, bf(ffn_w_in[i]), ffn_conv_w[i], ffn_conv_b[i], tm=512, tn=1408)
        x = matmul_residual_ln([act.reshape(T, -1)], bf(ffn_w_out[i]), x.reshape(T, D),
                               ln_g[i, 2], ln_b[i, 2], alpha, tm=512).reshape(B, S, D)
    return x
```

```python
import functools
import math

import jax
import jax.numpy as jnp
from jax import lax
from jax.experimental import pallas as pl
from jax.experimental.pallas import tpu as pltpu

BF16 = jnp.bfloat16
F32 = jnp.float32

HEAD_DIM = 64
N_MIXERS = 3
DIL_PATTERNS = ((128, 1), (512, 4), (2048, 16))
XATT_HEADS = 4
LN_EPS = 1e-5
GN_EPS = 64e-5
NEG_INF = -1e30
WKV_CHUNK = 64
VMEM_LIMIT_BYTES = 56 * 1024 * 1024
LOG2E = math.log2(math.e)

NT_DIMS = (((1,), (1,)), ((), ()))
TN_DIMS = (((0,), (0,)), ((), ()))


def _params(*semantics):
    return pltpu.CompilerParams(dimension_semantics=semantics, vmem_limit_bytes=VMEM_LIMIT_BYTES)


def _dot(a, b):
    return jnp.dot(a, b, preferred_element_type=F32)


def _dot_nt(a, b, precision=None):
    return lax.dot_general(a, b, NT_DIMS, preferred_element_type=F32, precision=precision)


def _layer_norm(y, g, b):
    mu = jnp.mean(y, -1, keepdims=True)
    yc = y - mu
    var = jnp.mean(yc * yc, -1, keepdims=True)
    return yc * lax.rsqrt(var + LN_EPS) * g + b


def _run_pipelined(chains, skew):
    results = [None] * len(chains)
    live = [True] * len(chains)
    tick = 0
    while any(live):
        for n, chain in enumerate(chains):
            if live[n] and tick >= n * skew:
                try:
                    next(chain)
                except StopIteration as stop:
                    results[n] = stop.value
                    live[n] = False
        tick += 1
    return results


MM_COL_CHUNK = 512


def _mm_kernel(x_ref, w_ref, *rest):
    o_ref = rest[-1]
    x = x_ref[...].astype(BF16)
    for c0 in range(0, o_ref.shape[1], MM_COL_CHUNK):
        cols = slice(c0, c0 + MM_COL_CHUNK)
        y = _dot(x, w_ref[:, cols])
        if len(rest) == 2:
            y = y * rest[0][:, cols]
        o_ref[:, cols] = y.astype(o_ref.dtype)


def matmul(x, w, out_dtype, tm, tn, col_scale=None):
    M, K = x.shape
    N = w.shape[1]
    in_specs = [pl.BlockSpec((tm, K), lambda j, i: (i, 0)), pl.BlockSpec((K, tn), lambda j, i: (0, j))]
    operands = [x, w]
    if col_scale is not None:
        in_specs.append(pl.BlockSpec((1, tn), lambda j, i: (0, j)))
        operands.append(col_scale.reshape(1, N))
    return pl.pallas_call(
        _mm_kernel,
        grid=(N // tn, M // tm),
        in_specs=in_specs,
        out_specs=pl.BlockSpec((tm, tn), lambda j, i: (i, j)),
        out_shape=jax.ShapeDtypeStruct((M, N), out_dtype),
        compiler_params=_params("arbitrary", "arbitrary"),
        name="matmul",
    )(*operands)


ROW_CHUNK = 256


def _mm_res_ln_kernel(*refs, n_h, alpha):
    h_refs = refs[:n_h]
    w_ref, x_ref, g_ref, b_ref, o_ref = refs[n_h:]

    def row_chunk(r0):
        rows = slice(r0, r0 + ROW_CHUNK)
        h = h_refs[0][rows, :]
        for r in h_refs[1:]:
            h = h + r[rows, :]
        hw = _dot(h.astype(BF16), w_ref[...])
        yield
        o_ref[rows, :] = _layer_norm(alpha * x_ref[rows, :] + hw, g_ref[...], b_ref[...])

    _run_pipelined([row_chunk(r0) for r0 in range(0, o_ref.shape[0], ROW_CHUNK)], skew=1)


def matmul_residual_ln(hs, w, x, g, b, alpha, tm):
    M, K = hs[0].shape
    D = w.shape[1]
    row = lambda i: (i, 0)
    fixed = lambda i: (0, 0)
    return pl.pallas_call(
        functools.partial(_mm_res_ln_kernel, n_h=len(hs), alpha=alpha),
        grid=(M // tm,),
        in_specs=[pl.BlockSpec((tm, K), row) for _ in hs]
        + [pl.BlockSpec((K, D), fixed), pl.BlockSpec((tm, D), row),
           pl.BlockSpec((1, D), fixed), pl.BlockSpec((1, D), fixed)],
        out_specs=pl.BlockSpec((tm, D), row),
        out_shape=jax.ShapeDtypeStruct((M, D), F32),
        compiler_params=_params("parallel"),
        name="matmul_residual_ln",
    )(*hs, w, x, g.reshape(1, D), b.reshape(1, D))


ALIBI_SPLIT = 3
ALIBI_FEATS = 4 * ALIBI_SPLIT


def _alibi_features(pos, lane0, cs, key_side):
    lane = lax.broadcasted_iota(jnp.int32, (1, 2 * HEAD_DIM), 1) - lane0
    kind = lane & 3
    piece = lane >> 2
    live = (lane >= 0) & (lane < ALIBI_FEATS)
    c = jnp.where(piece == 0, cs[0], jnp.where(piece == 1, cs[1], cs[2]))
    hi = (pos >> 6).astype(F32)
    lo = (pos & 63).astype(F32)
    if key_side:
        f = jnp.where(kind == 0, -64.0 * c, jnp.where(kind == 1, -c, jnp.where(kind == 2, 64.0 * hi, lo)))
    else:
        f = jnp.where(kind == 0, hi, jnp.where(kind == 1, lo, c))
    return jnp.where(live, f, 0.0)


def _diff_attn_kernel(cs_ref, lam_ref, q_ref, k_ref, v_ref, subln_ref, o_ref, kaug_ref, vaug_ref, s_ref,
                      *, tq, out_scale):
    h = pl.program_id(1)
    qi = pl.program_id(2)
    E = HEAD_DIM
    S = k_ref.shape[1]
    nb = S // tq
    cs = [cs_ref[ALIBI_SPLIT * h + m] for m in range(ALIBI_SPLIT)]
    lane = lax.broadcasted_iota(jnp.int32, (1, 2 * E), 1)
    first = lane < E

    @pl.when(qi == 0)
    def _():
        k = k_ref[0]
        pos = lax.broadcasted_iota(jnp.int32, (S, 1), 0)
        for c, lane0 in enumerate((E, 0)):
            feat = _alibi_features(pos, lane0, cs, key_side=True)
            own = first if c == 0 else jnp.logical_not(first)
            kaug_ref[c, 0] = jnp.where(own, k, feat.astype(BF16))
            kaug_ref[c, 1] = jnp.where(own, k, (-feat).astype(BF16))
        ones = jnp.where(lane == 0, 1.0, 0.0).astype(BF16)
        vaug_ref[...] = jnp.concatenate([v_ref[0], jnp.broadcast_to(ones, (S, 2 * E))], 1)

    start = pl.multiple_of(qi * tq, tq)
    half = tq // 2

    def block_max(t):
        bm = t[:, :2 * E]
        for g in range(1, tq // (2 * E)):
            bm = jnp.maximum(bm, t[:, g * 2 * E:(g + 1) * 2 * E])
        return bm

    def softmax_map(c, r0):
        rows = slice(r0, r0 + half)
        own = first if c == 0 else jnp.logical_not(first)
        pos_q = qi * tq + r0 + lax.broadcasted_iota(jnp.int32, (half, 1), 0)
        q = q_ref[0, rows, :]
        qa = jnp.where(own, q, _alibi_features(pos_q, E if c == 0 else 0, cs, key_side=False).astype(BF16))
        lane_max = jnp.full((half, 2 * E), NEG_INF, F32)
        for jb in range(nb):
            side = (jb > qi).astype(jnp.int32)
            t = _dot_nt(qa, kaug_ref[c, side, jb * tq:(jb + 1) * tq, :])
            s_ref[c, jb, rows, :] = t
            lane_max = jnp.maximum(lane_max, jnp.where(jb == qi, NEG_INF, block_max(t)))
            yield
        t = jnp.minimum(s_ref[c, qi, rows, :], _dot_nt(qa, kaug_ref[c, 1, pl.ds(start, tq), :]))
        s_ref[c, qi, rows, :] = t
        m = jnp.max(jnp.maximum(lane_max, block_max(t)), -1, keepdims=True)
        yield
        ps = []
        for jb in range(nb):
            ps.append(jnp.exp2(s_ref[c, jb, rows, :] - m).astype(BF16))
            yield
        ol = jnp.zeros((half, 4 * E), F32)
        for jb in range(nb):
            ol = ol + _dot(ps[jb], vaug_ref[jb * tq:(jb + 1) * tq, :])
            yield
        return ol[:, :2 * E] * (1.0 / ol[:, 2 * E:2 * E + 1])

    items = [(c, r0) for r0 in (0, half) for c in (0, 1)]
    results = _run_pipelined([softmax_map(c, r0) for c, r0 in items], skew=nb + 1)
    for n, r0 in enumerate((0, half)):
        o = results[2 * n] - lam_ref[0] * results[2 * n + 1]
        o = o * lax.rsqrt(jnp.mean(o * o, -1, keepdims=True) + LN_EPS) * subln_ref[...] * out_scale
        o_ref[0, r0:r0 + half, :] = o.astype(o_ref.dtype)


def diff_attention_core(qkv, slope_pieces, lam_full, subln, lambda_init, tq):
    B, S, D3 = qkv.shape
    D = D3 // 3
    H = D // (2 * HEAD_DIM)
    W = 2 * HEAD_DIM
    smem = pl.BlockSpec(memory_space=pltpu.SMEM)
    return pl.pallas_call(
        functools.partial(_diff_attn_kernel, tq=tq, out_scale=1.0 - lambda_init),
        grid=(B, H, S // tq),
        in_specs=[smem, smem,
                  pl.BlockSpec((1, tq, W), lambda b, h, i: (b, i, h)),
                  pl.BlockSpec((1, S, W), lambda b, h, i: (b, 0, H + h)),
                  pl.BlockSpec((1, S, W), lambda b, h, i: (b, 0, 2 * H + h)),
                  pl.BlockSpec((1, W), lambda b, h, i: (0, 0))],
        out_specs=pl.BlockSpec((1, tq, W), lambda b, h, i: (b, i, h)),
        out_shape=jax.ShapeDtypeStruct((B, S, D), BF16),
        scratch_shapes=[pltpu.VMEM((2, 2, S, W), BF16), pltpu.VMEM((S, 2 * W), BF16),
                        pltpu.VMEM((2, S // tq, tq, tq), F32)],
        compiler_params=_params("parallel", "parallel", "arbitrary"),
        name="diff_attention",
    )(slope_pieces, lam_full, qkv, qkv, qkv, subln.reshape(1, W))


def _split_slopes(slopes):
    pieces, rest = [], slopes
    for _ in range(ALIBI_SPLIT):
        piece = rest.astype(BF16).astype(F32)
        pieces.append(piece)
        rest = rest - piece
    return jnp.stack(pieces, -1).reshape(-1)


DIL_STEP_TOKENS = 2048


def _dil_attn_kernel(slope_ref, q_ref, k_ref, v_ref, o_ref, lse_ref, *, tq, half, dilation, nq):
    hp = pl.program_id(1)
    blk = pl.program_id(2)
    E = HEAD_DIM
    L = k_ref.shape[1] // dilation
    kw = tq + 2 * half
    lane = lax.broadcasted_iota(jnp.int32, (tq, 2 * E), 1)
    first = lane < E
    zero = jnp.zeros((tq, 2 * E), BF16)
    slope0 = slope_ref[2 * hp]
    slope1 = slope_ref[2 * hp + 1]
    row = lax.broadcasted_iota(jnp.int32, (tq, 1), 0)
    colk = lax.broadcasted_iota(jnp.int32, (1, kw), 1)

    def rows(start, size):
        return pl.ds(start, size) if dilation == 1 else pl.ds(start, size, stride=dilation)

    def tile(j, r):
        uq = (blk * nq + j) * tq
        ks = jnp.clip(uq - half, 0, L - kw)
        local = rows(j * tq * dilation + r, tq)
        window = rows(ks * dilation + r, kw)
        q = (q_ref[0, local, :] * (E ** -0.5)).astype(BF16)
        k = k_ref[0, window, :].astype(BF16)
        scores = [_dot_nt(jnp.where(first, q, zero), k), _dot_nt(jnp.where(first, zero, q), k)]
        yield
        rel = jnp.abs((ks + colk) - (uq + row))
        dist = (dilation * rel).astype(F32)
        probs, invs, lses = [], [], []
        for s, slope in zip(scores, (slope0, slope1)):
            s = jnp.where(rel <= half, s - slope * dist, NEG_INF)
            m = jnp.max(s, -1, keepdims=True)
            p = jnp.exp(s - m)
            l = jnp.sum(p, -1, keepdims=True)
            probs.append(p.astype(BF16))
            invs.append(1.0 / l)
            lses.append(m + jnp.log(l))
        yield
        v = v_ref[0, window, :].astype(BF16)
        o0 = _dot(probs[0], v) * invs[0]
        o1 = _dot(probs[1], v) * invs[1]
        o_ref[0, local, :] = jnp.where(first, o0, o1)
        lse_ref[0, local, :] = jnp.where(first, lses[0], lses[1])

    _run_pipelined([tile(j, r) for j in range(nq) for r in range(dilation)], skew=1)


def dilated_group_core(qkv, slopes, g, window, dilation, tq):
    B, S, C = qkv.shape
    G = len(DIL_PATTERNS)
    D = C // (3 * G)
    W = 2 * HEAD_DIM
    HP = D // W
    half = window // (2 * dilation)
    base = g * 3 * HP
    nq = DIL_STEP_TOKENS // (tq * dilation)
    smem = pl.BlockSpec(memory_space=pltpu.SMEM)
    out_spec = pl.BlockSpec((1, DIL_STEP_TOKENS, W), lambda b, hp, i: (b, i, hp))
    o, lse = pl.pallas_call(
        functools.partial(_dil_attn_kernel, tq=tq, half=half, dilation=dilation, nq=nq),
        grid=(B, HP, S // DIL_STEP_TOKENS),
        in_specs=[smem,
                  pl.BlockSpec((1, DIL_STEP_TOKENS, W), lambda b, hp, i: (b, i, base + hp)),
                  pl.BlockSpec((1, S, W), lambda b, hp, i: (b, 0, base + HP + hp)),
                  pl.BlockSpec((1, S, W), lambda b, hp, i: (b, 0, base + 2 * HP + hp))],
        out_specs=[out_spec, out_spec],
        out_shape=[jax.ShapeDtypeStruct((B, S, D), F32)] * 2,
        compiler_params=_params("parallel", "parallel", "arbitrary"),
        name=f"dilated_attention_d{dilation}",
    )(slopes, qkv, qkv, qkv)
    return o.reshape(B * S, D), lse.reshape(B * S, D)


def _dil_merge_kernel(o0, o1, o2, l0, l1, l2, out_ref):
    a, b, c = l0[...], l1[...], l2[...]
    m = jnp.maximum(jnp.maximum(a, b), c)
    ea, eb, ec = jnp.exp(a - m), jnp.exp(b - m), jnp.exp(c - m)
    inv = 1.0 / (ea + eb + ec)
    out_ref[...] = ((ea * inv) * o0[...] + (eb * inv) * o1[...] + (ec * inv) * o2[...]).astype(out_ref.dtype)


def dilated_merge(outs, lses, tm):
    M, D = outs[0].shape
    spec = pl.BlockSpec((tm, D), lambda i: (i, 0))
    return pl.pallas_call(
        _dil_merge_kernel,
        grid=(M // tm,),
        in_specs=[spec] * 6,
        out_specs=spec,
        out_shape=jax.ShapeDtypeStruct((M, D), BF16),
        compiler_params=_params("parallel"),
        name="dilated_merge",
    )(*outs, *lses)


def _shifted_rows(x, prev_ref, next_ref, i, n_i, reverse):
    tm = x.shape[0]
    r = lax.broadcasted_iota(jnp.int32, (tm, 1), 0)
    if reverse:
        edge = jnp.where(i == n_i - 1, 0.0, next_ref[0, 0:1, :])
        return jnp.where(r == tm - 1, edge, pltpu.roll(x, tm - 1, 0))
    edge = jnp.where(i == 0, 0.0, prev_ref[0, 7:8, :])
    return jnp.where(r == 0, edge, pltpu.roll(x, 1, 0))


def _rwkv_proj_kernel(x_ref, prev_ref, next_ref, mu_ref, wr_ref, wk_ref, wv_ref, w0_ref, w1_ref, w2_ref,
                      a0_ref, a1_ref, a2_ref, g1_ref, g2_ref,
                      r_ref, k_ref, v_ref, lw_ref, a_ref, g_ref, *, reverse):
    i = pl.program_id(1)
    x = x_ref[0]
    xx = _shifted_rows(x, prev_ref, next_ref, i, pl.num_programs(1), reverse) - x

    def projection(j, w_in, finish, out_ref):
        xm = (x + xx * mu_ref[j:j + 1, :]).astype(BF16)
        yield
        y = _dot(xm, w_in[...])
        yield
        out_ref[0] = finish(y).astype(out_ref.dtype)

    def decay(y):
        z = -(w0_ref[...] + _dot(jnp.tanh(y).astype(BF16), w2_ref[...]))
        softplus = jnp.maximum(z, 0.0) + jnp.log(1.0 + jnp.exp(-jnp.abs(z)))
        return -jnp.exp(-softplus - 0.5)

    def rate(y):
        return jax.nn.sigmoid(a0_ref[...] + _dot(y.astype(BF16), a2_ref[...]))

    def gate(y):
        return _dot(jax.nn.sigmoid(y).astype(BF16), g2_ref[...])

    same = lambda y: y
    _run_pipelined([projection(0, wr_ref, same, r_ref), projection(1, w1_ref, decay, lw_ref),
                    projection(2, wk_ref, same, k_ref), projection(4, a1_ref, rate, a_ref),
                    projection(3, wv_ref, same, v_ref), projection(5, g1_ref, gate, g_ref)], skew=1)


def rwkv_projections(x, mu, w_rkv, w0, w1, w2, a0, a1, a2, g1, g2, reverse, tm):
    B, S, D = x.shape
    nb = tm // 8
    fixed2 = lambda b, i: (0, 0)
    tile = pl.BlockSpec((1, tm, D), lambda b, i: (b, i, 0))
    full = lambda a: pl.BlockSpec(a.shape, fixed2)
    ops = (mu, w_rkv[0], w_rkv[1], w_rkv[2], w0.reshape(1, D), w1, w2, a0.reshape(1, D), a1, a2, g1, g2)
    return pl.pallas_call(
        functools.partial(_rwkv_proj_kernel, reverse=reverse),
        grid=(B, S // tm),
        in_specs=[tile,
                  pl.BlockSpec((1, 8, D), lambda b, i: (b, jnp.maximum(i * nb - 1, 0), 0)),
                  pl.BlockSpec((1, 8, D), lambda b, i: (b, jnp.minimum((i + 1) * nb, S // 8 - 1), 0))]
        + [full(a) for a in ops],
        out_specs=[tile] * 6,
        out_shape=[jax.ShapeDtypeStruct((B, S, D), dt) for dt in (BF16, BF16, BF16, F32, F32, BF16)],
        compiler_params=_params("parallel", "arbitrary"),
        name="rwkv_projections_bwd" if reverse else "rwkv_projections_fwd",
    )(x, x, x, *ops)


SLAB_HEADS = 4
SLAB = SLAB_HEADS * HEAD_DIM


def _split2(x):
    hi = x.astype(BF16)
    lo = (x - hi.astype(F32)).astype(BF16)
    return hi, lo


def _block_diag(x, block_mask):
    xb = x.astype(BF16)
    return jnp.where(block_mask, jnp.concatenate([xb] * SLAB_HEADS, 0), jnp.zeros((), BF16))


def _per_head_dot(lhs, x, block_mask, nt=False):
    mm = _dot_nt if nt else _dot
    return mm(lhs.astype(BF16), _block_diag(x, block_mask))


def _per_head_sum(x, ones_bd):
    M = x.shape[0]
    s = _dot(jnp.concatenate(_split2(x), 0), ones_bd)
    return s[:M] + s[M:]


def _wkv_slab_chunk(state, r, k, v, cum, lw, a, kkw, kaw, rkw, gng, gnb, gate, reverse, consts):
    block_mask, ones_bd, strict, incl, eye, lane_head = consts
    C = r.shape[0]
    kk = k * kkw
    k = k * (1.0 + (a - 1.0) * kaw)
    sums = _per_head_sum(jnp.concatenate([kk * kk, r * k * rkw], 0), ones_bd)
    yield
    kk = kk * lax.rsqrt(jnp.maximum(sums[:C], 1e-24))
    bonus = sums[C:]
    dec = jnp.exp(cum)
    inv = jnp.exp(-cum)
    alpha_bar = -kk * jnp.exp(cum - lw)
    beta_t = kk * a * inv
    k_t = k * inv
    r_bar = r * dec

    ar = jnp.concatenate([alpha_bar, r_bar], 0)
    gb = _per_head_dot(ar, beta_t, block_mask, nt=True)
    gk = _per_head_dot(ar, k_t, block_mask, nt=True)
    yield
    a_ab = jnp.where(strict, gb[:C], 0.0)
    a_rb = jnp.where(incl, gb[C:], 0.0)
    a_ak = jnp.where(strict, gk[:C], 0.0)
    a_rk = jnp.where(incl, gk[C:], 0.0)

    x = eye + a_ab
    p = _per_head_dot(a_ab, a_ab, block_mask)
    av = _per_head_dot(jnp.concatenate([a_ak, a_rk], 0), v, block_mask)
    ss = _per_head_dot(ar, state, block_mask, nt=True)
    yield
    m = 2
    while 2 * m < C:
        xp = _per_head_dot(jnp.concatenate([x, p], 0), p, block_mask)
        yield
        x = x + xp[:C]
        p = xp[C:]
        m *= 2
    xp = _per_head_dot(x, p, block_mask)
    yield
    x = x + xp
    u = _per_head_dot(x, ss[:C] + av[:C], block_mask)
    yield
    y = ss[C:] + _per_head_dot(a_rb, u, block_mask) + av[C:]

    total = cum[0:1] if reverse else cum[C - 1:C]
    end = jnp.exp(total)
    cross = lax.dot_general(jnp.concatenate([u, v], 0).astype(BF16),
                            jnp.concatenate([beta_t * end, k_t * end], 0).astype(BF16),
                            TN_DIMS, preferred_element_type=F32)
    mu = _per_head_sum(y, ones_bd) * (1.0 / HEAD_DIM)
    yield
    new_state = state * end
    for hh in range(SLAB_HEADS):
        new_state = new_state + jnp.where(lane_head == hh, cross[hh * HEAD_DIM:(hh + 1) * HEAD_DIM], 0.0)

    yc = y - mu
    var = _per_head_sum(yc * yc, ones_bd) * (1.0 / HEAD_DIM)
    yield
    out = yc * lax.rsqrt(var + GN_EPS) * gng + gnb + bonus * v
    return new_state, out * gate


def _wkv_kernel(rf, kf, vf, lwf, af, gf, rb, kb, vb, lwb, ab, gb,
                kk_ref, ka_ref, rk_ref, gng_ref, gnb_ref, of_ref, ob_ref, state_ref):
    c = pl.program_id(1)
    C = rf.shape[1]
    D = rf.shape[2]
    N = HEAD_DIM

    @pl.when(c == 0)
    def _():
        state_ref[...] = jnp.zeros_like(state_ref)

    row = lax.broadcasted_iota(jnp.int32, (SLAB, SLAB), 0)
    col = lax.broadcasted_iota(jnp.int32, (SLAB, SLAB), 1)
    block_mask = (row // N) == (col // N)
    ones_bd = jnp.where(block_mask, 1.0, 0.0).astype(BF16)
    t = lax.broadcasted_iota(jnp.int32, (C, SLAB), 0)
    lane = lax.broadcasted_iota(jnp.int32, (C, SLAB), 1)
    s = lane % C
    lane_head = lax.broadcasted_iota(jnp.int32, (N, SLAB), 1) // N
    ti = lax.broadcasted_iota(jnp.int32, (C, C), 0)
    si = lax.broadcasted_iota(jnp.int32, (C, C), 1)
    eye = (s == t).astype(F32)

    chains = []
    for d, (refs, o_ref) in enumerate((((rf, kf, vf, lwf, af, gf), of_ref), ((rb, kb, vb, lwb, ab, gb), ob_ref))):
        reverse = d == 1
        strict = (s > t) if reverse else (s < t)
        incl = (s >= t) if reverse else (s <= t)
        consts = (block_mask, ones_bd, strict, incl, eye, lane_head)
        tri = ((si >= ti) if reverse else (si <= ti)).astype(BF16)
        lw_all = refs[3][0]
        parts = _dot(tri, jnp.concatenate(_split2(lw_all), 1))
        cum_all = parts[:, :D] + parts[:, D:]
        for sb in range(D // SLAB):
            sl = slice(sb * SLAB, (sb + 1) * SLAB)
            r, k, v, lw, a, gate = (ref[0, :, sl].astype(F32) for ref in refs)
            chains.append((d, sb, sl, o_ref, _wkv_slab_chunk(
                state_ref[d, sb], r, k, v, cum_all[:, sl], lw, a,
                kk_ref[d:d + 1, sl], ka_ref[d:d + 1, sl], rk_ref[:, sl], gng_ref[:, sl], gnb_ref[:, sl],
                gate, reverse, consts)))

    results = _run_pipelined([chain for *_, chain in chains], skew=0)
    for (d, sb, sl, o_ref, _), (new_state, out) in zip(chains, results):
        state_ref[d, sb] = new_state
        o_ref[0, :, sl] = out


def rwkv_scan(fwd, bwd, k_k, k_a, r_k, gn_g, gn_b):
    B, S, D = fwd[0].shape
    C = WKV_CHUNK
    NC = S // C
    f_spec = pl.BlockSpec((1, C, D), lambda b, c: (b, c, 0))
    b_spec = pl.BlockSpec((1, C, D), lambda b, c: (b, NC - 1 - c, 0))
    two = pl.BlockSpec((2, D), lambda b, c: (0, 0))
    one = pl.BlockSpec((1, D), lambda b, c: (0, 0))
    return pl.pallas_call(
        _wkv_kernel,
        grid=(B, NC),
        in_specs=[f_spec] * 6 + [b_spec] * 6 + [two, two, one, one, one],
        out_specs=[f_spec, b_spec],
        out_shape=[jax.ShapeDtypeStruct((B, S, D), F32)] * 2,
        scratch_shapes=[pltpu.VMEM((2, D // SLAB, HEAD_DIM, SLAB), F32)],
        compiler_params=_params("parallel", "arbitrary"),
        name="rwkv_scan",
    )(*fwd, *bwd, k_k, k_a, r_k.reshape(1, D), gn_g.reshape(1, D), gn_b.reshape(1, D))


def _xatt_kernel(x_ref, kv_ref, wq_ref, wo_ref, g_ref, b_ref, o_ref, *, alpha):
    D = x_ref.shape[-1]
    E = D // XATT_HEADS

    def row_chunk(r0):
        rows = slice(r0, r0 + ROW_CHUNK)
        q = (_dot(x_ref[0, rows, :].astype(BF16), wq_ref[...]) * (E ** -0.5)).astype(BF16)
        yield
        scores = [_dot_nt(q[:, h * E:(h + 1) * E], kv_ref[0, :, h * E:(h + 1) * E]) for h in range(XATT_HEADS)]
        yield
        probs, invs = [], []
        for s in scores:
            p = jnp.exp(s - jnp.max(s, -1, keepdims=True))
            invs.append(1.0 / jnp.sum(p, -1, keepdims=True))
            probs.append(p.astype(BF16))
        yield
        outs = [(_dot(probs[h], kv_ref[0, :, D + h * E:D + (h + 1) * E]) * invs[h]).astype(BF16)
                for h in range(XATT_HEADS)]
        yield
        y = _dot(jnp.concatenate(outs, -1), wo_ref[...])
        yield
        o_ref[0, rows, :] = _layer_norm(alpha * x_ref[0, rows, :] + y, g_ref[...], b_ref[...])

    _run_pipelined([row_chunk(r0) for r0 in range(0, o_ref.shape[1], ROW_CHUNK)], skew=1)


def cross_attention_sublayer(x, kv, w_q, w_o, g, b, alpha, tm):
    B, S, D = x.shape
    M = kv.shape[1]
    fixed = lambda bb, i: (0, 0)
    tile = pl.BlockSpec((1, tm, D), lambda bb, i: (bb, i, 0))
    return pl.pallas_call(
        functools.partial(_xatt_kernel, alpha=alpha),
        grid=(B, S // tm),
        in_specs=[tile, pl.BlockSpec((1, M, 2 * D), lambda bb, i: (bb, 0, 0)),
                  pl.BlockSpec((D, D), fixed), pl.BlockSpec((D, D), fixed),
                  pl.BlockSpec((1, D), fixed), pl.BlockSpec((1, D), fixed)],
        out_specs=tile,
        out_shape=jax.ShapeDtypeStruct((B, S, D), F32),
        compiler_params=_params("parallel", "arbitrary"),
        name="cross_attention",
    )(x, kv, w_q, w_o, g.reshape(1, D), b.reshape(1, D))


FFN_LANES = 128


def _ffn_in_kernel(x_ref, prev_ref, next_ref, wg_ref, wv_ref, cw_ref, cb_ref, o_ref, w_ref):
    i = pl.program_id(2)
    n_i = pl.num_programs(2)
    tm = x_ref.shape[1]
    C = FFN_LANES
    n_chunks = o_ref.shape[2] // C

    @pl.when((pl.program_id(1) == 0) & (i == 0))
    def _():
        for n in range(n_chunks):
            w_ref[:, 2 * C * n:2 * C * n + C] = wg_ref[:, C * n:C * (n + 1)]
            w_ref[:, 2 * C * n + C:2 * C * (n + 1)] = wv_ref[:, C * n:C * (n + 1)]

    x = jnp.concatenate([prev_ref[0], x_ref[0], next_ref[0]], 0).astype(BF16)
    r = lax.broadcasted_iota(jnp.int32, (tm, 1), 0)

    def column_chunk(n):
        y = _dot(x, w_ref[:, 2 * C * n:2 * C * (n + 1)])
        yield
        gate = y[8:8 + tm, :C]
        val = y[8:8 + tm, C:]
        g_prev = jnp.where(i == 0, 0.0, y[7:8, :C])
        g_next = jnp.where(i == n_i - 1, 0.0, y[tm + 8:tm + 9, :C])
        before = jnp.where(r == 0, g_prev, pltpu.roll(gate, 1, 0))
        after = jnp.where(r == tm - 1, g_next, pltpu.roll(gate, tm - 1, 0))
        cols = slice(C * n, C * (n + 1))
        conv = before * cw_ref[0:1, cols] + gate * cw_ref[1:2, cols] + after * cw_ref[2:3, cols] + cb_ref[:, cols]
        gelu = 0.5 * conv * (1.0 + lax.erf(conv * (2.0 ** -0.5)))
        o_ref[0, :, cols] = (gelu * val).astype(o_ref.dtype)

    _run_pipelined([column_chunk(n) for n in range(n_chunks)], skew=1)


def ffn_in(x, w_in, conv_w, conv_b, tm, tn):
    B, S, D = x.shape
    F = w_in.shape[1] // 2
    nb = tm // 8
    nj = F // tn
    return pl.pallas_call(
        _ffn_in_kernel,
        grid=(nj, B, S // tm),
        in_specs=[pl.BlockSpec((1, tm, D), lambda j, b, i: (b, i, 0)),
                  pl.BlockSpec((1, 8, D), lambda j, b, i: (b, jnp.maximum(i * nb - 1, 0), 0)),
                  pl.BlockSpec((1, 8, D), lambda j, b, i: (b, jnp.minimum((i + 1) * nb, S // 8 - 1), 0)),
                  pl.BlockSpec((D, tn), lambda j, b, i: (0, j)),
                  pl.BlockSpec((D, tn), lambda j, b, i: (0, nj + j)),
                  pl.BlockSpec((3, tn), lambda j, b, i: (0, j)),
                  pl.BlockSpec((1, tn), lambda j, b, i: (0, j))],
        out_specs=pl.BlockSpec((1, tm, tn), lambda j, b, i: (b, i, j)),
        out_shape=jax.ShapeDtypeStruct((B, S, F), BF16),
        scratch_shapes=[pltpu.VMEM((D, 2 * tn), BF16)],
        compiler_params=_params("arbitrary", "arbitrary", "arbitrary"),
        name="ffn_in",
    )(x, x, x, w_in, w_in, conv_w, conv_b.reshape(1, F))


def _alibi_slopes(n):
    return jnp.exp2(-8.0 * jnp.arange(1, n + 1, dtype=F32) / n)


def kernel(x, mem, diff_w_qkv, diff_lambda, diff_subln, diff_w_o, dil_w_qkv, dil_w_o, rwkv_mu, rwkv_w_rkv, rwkv_w0, rwkv_w1, rwkv_w2, rwkv_a0, rwkv_a1, rwkv_a2, rwkv_g1, rwkv_g2, rwkv_k_k, rwkv_k_a, rwkv_r_k, rwkv_gn_g, rwkv_gn_b, rwkv_w_o, xatt_w_q, xatt_w_kv, xatt_w_o, ffn_w_in, ffn_conv_w, ffn_conv_b, ffn_w_out, ln_g, ln_b):
    B, S, D = x.shape
    depth = xatt_w_q.shape[0]
    alpha = (2 * depth) ** 0.25
    T = B * S
    bf = lambda a: a.astype(BF16)
    mem2 = mem.reshape(-1, D)

    for i in range(depth):
        m, j = i % N_MIXERS, i // N_MIXERS
        x2 = x.reshape(T, D)
        if m == 0:
            lambda_init = 0.8 - 0.6 * math.exp(-0.3 * i)
            lamf = diff_lambda[j].astype(F32)
            lam_full = (jnp.exp(jnp.sum(lamf[0] * lamf[1])) - jnp.exp(jnp.sum(lamf[2] * lamf[3]))
                        + lambda_init).reshape(1)
            q_scale = jnp.where(jnp.arange(3 * D) < D, LOG2E * HEAD_DIM ** -0.5, 1.0).astype(F32)
            qkv = matmul(x2, bf(diff_w_qkv[j]), BF16, tm=1024, tn=3 * D, col_scale=q_scale).reshape(B, S, 3 * D)
            o = diff_attention_core(qkv, _split_slopes(LOG2E * _alibi_slopes(D // (2 * HEAD_DIM))), lam_full,
                                    diff_subln[j], lambda_init, tq=512)
            hs, w_o = [o.reshape(T, D)], diff_w_o[j]
        elif m == 1:
            qkv = matmul(x2, bf(dil_w_qkv[j]), F32, tm=512, tn=3 * D).reshape(B, S, -1)
            slopes = _alibi_slopes(D // HEAD_DIM)
            parts = [dilated_group_core(qkv, slopes, g, window, dilation, tq=128)
                     for g, (window, dilation) in enumerate(DIL_PATTERNS)]
            hs = [dilated_merge([p[0] for p in parts], [p[1] for p in parts], tm=512)]
            w_o = dil_w_o[j]
        else:
            dirs = [rwkv_projections(x, rwkv_mu[j, d], bf(rwkv_w_rkv[j]), rwkv_w0[j, d], bf(rwkv_w1[j, d]),
                                     bf(rwkv_w2[j, d]), rwkv_a0[j, d], bf(rwkv_a1[j, d]), bf(rwkv_a2[j, d]),
                                     bf(rwkv_g1[j, d]), bf(rwkv_g2[j, d]), reverse=d == 1, tm=512)
                    for d in range(2)]
            yf, yb = rwkv_scan(dirs[0], dirs[1], rwkv_k_k[j], rwkv_k_a[j], rwkv_r_k[j],
                               rwkv_gn_g[j], rwkv_gn_b[j])
            hs, w_o = [yf.reshape(T, D), yb.reshape(T, D)], rwkv_w_o[j]
        x2 = matmul_residual_ln(hs, bf(w_o), x2, ln_g[i, 0], ln_b[i, 0], alpha, tm=1024)

        kv = matmul(mem2, bf(xatt_w_kv[i]), BF16, tm=512, tn=2 * D).reshape(B, -1, 2 * D)
        x = cross_attention_sublayer(x2.reshape(B, S, D), kv, bf(xatt_w_q[i]), bf(xatt_w_o[i]),
                                     ln_g[i, 1], ln_b[i, 1], alpha, tm=1024)

        act = ffn_in(x, bf(ffn_w_in[i]), ffn_conv_w[i], ffn_conv_b[i], tm=512, tn=1408)
        x = matmul_residual_ln([act.reshape(T, -1)], bf(ffn_w_out[i]), x.reshape(T, D),
                               ln_g[i, 2], ln_b[i, 2], alpha, tm=1024).reshape(B, S, D)
    return x
```

```python
import functools
import math

import jax
import jax.numpy as jnp
from jax import lax
from jax.experimental import pallas as pl
from jax.experimental.pallas import tpu as pltpu

BF16 = jnp.bfloat16
F32 = jnp.float32

HEAD_DIM = 64
N_MIXERS = 3
DIL_PATTERNS = ((128, 1), (512, 4), (2048, 16))
XATT_HEADS = 4
LN_EPS = 1e-5
GN_EPS = 64e-5
NEG_INF = -1e30
WKV_CHUNK = 64
VMEM_LIMIT_BYTES = 56 * 1024 * 1024
LOG2E = math.log2(math.e)

NT_DIMS = (((1,), (1,)), ((), ()))
TN_DIMS = (((0,), (0,)), ((), ()))


def _params(*semantics):
    return pltpu.CompilerParams(dimension_semantics=semantics, vmem_limit_bytes=VMEM_LIMIT_BYTES)


def _dot(a, b):
    return jnp.dot(a, b, preferred_element_type=F32)


def _dot_nt(a, b, precision=None):
    return lax.dot_general(a, b, NT_DIMS, preferred_element_type=F32, precision=precision)


def _layer_norm(y, g, b):
    mu = jnp.mean(y, -1, keepdims=True)
    yc = y - mu
    var = jnp.mean(yc * yc, -1, keepdims=True)
    return yc * lax.rsqrt(var + LN_EPS) * g + b


def _run_pipelined(chains, skew):
    results = [None] * len(chains)
    live = [True] * len(chains)
    tick = 0
    while any(live):
        for n, chain in enumerate(chains):
            if live[n] and tick >= n * skew:
                try:
                    next(chain)
                except StopIteration as stop:
                    results[n] = stop.value
                    live[n] = False
        tick += 1
    return results


MM_COL_CHUNK = 512


def _mm_kernel(x_ref, w_ref, *rest):
    o_ref = rest[-1]
    x = x_ref[...].astype(BF16)
    for c0 in range(0, o_ref.shape[1], MM_COL_CHUNK):
        cols = slice(c0, c0 + MM_COL_CHUNK)
        y = _dot(x, w_ref[:, cols])
        if len(rest) == 2:
            y = y * rest[0][:, cols]
        o_ref[:, cols] = y.astype(o_ref.dtype)


def matmul(x, w, out_dtype, tm, tn, col_scale=None):
    M, K = x.shape
    N = w.shape[1]
    in_specs = [pl.BlockSpec((tm, K), lambda j, i: (i, 0)), pl.BlockSpec((K, tn), lambda j, i: (0, j))]
    operands = [x, w]
    if col_scale is not None:
        in_specs.append(pl.BlockSpec((1, tn), lambda j, i: (0, j)))
        operands.append(col_scale.reshape(1, N))
    return pl.pallas_call(
        _mm_kernel,
        grid=(N // tn, M // tm),
        in_specs=in_specs,
        out_specs=pl.BlockSpec((tm, tn), lambda j, i: (i, j)),
        out_shape=jax.ShapeDtypeStruct((M, N), out_dtype),
        compiler_params=_params("arbitrary", "arbitrary"),
        name="matmul",
    )(*operands)


ROW_CHUNK = 256


def _mm_res_ln_kernel(*refs, n_h, alpha):
    h_refs = refs[:n_h]
    w_ref, x_ref, g_ref, b_ref, o_ref = refs[n_h:]

    def row_chunk(r0):
        rows = slice(r0, r0 + ROW_CHUNK)
        h = h_refs[0][rows, :]
        for r in h_refs[1:]:
            h = h + r[rows, :]
        hw = _dot(h.astype(BF16), w_ref[...])
        yield
        o_ref[rows, :] = _layer_norm(alpha * x_ref[rows, :] + hw, g_ref[...], b_ref[...])

    _run_pipelined([row_chunk(r0) for r0 in range(0, o_ref.shape[0], ROW_CHUNK)], skew=1)


def matmul_residual_ln(hs, w, x, g, b, alpha, tm):
    M, K = hs[0].shape
    D = w.shape[1]
    row = lambda i: (i, 0)
    fixed = lambda i: (0, 0)
    return pl.pallas_call(
        functools.partial(_mm_res_ln_kernel, n_h=len(hs), alpha=alpha),
        grid=(M // tm,),
        in_specs=[pl.BlockSpec((tm, K), row) for _ in hs]
        + [pl.BlockSpec((K, D), fixed), pl.BlockSpec((tm, D), row),
           pl.BlockSpec((1, D), fixed), pl.BlockSpec((1, D), fixed)],
        out_specs=pl.BlockSpec((tm, D), row),
        out_shape=jax.ShapeDtypeStruct((M, D), F32),
        compiler_params=_params("parallel"),
        name="matmul_residual_ln",
    )(*hs, w, x, g.reshape(1, D), b.reshape(1, D))


ALIBI_SPLIT = 3
ALIBI_FEATS = 4 * ALIBI_SPLIT


def _alibi_lanes(lane0):
    lane = lax.broadcasted_iota(jnp.int32, (1, 2 * HEAD_DIM), 1) - lane0
    return lane & 3, lane >> 2, (lane >= 0) & (lane < ALIBI_FEATS)


def _alibi_features(pos, lane0, cs, key_side):
    kind, piece, live = _alibi_lanes(lane0)
    hi = (pos >> 6).astype(F32)
    lo = (pos & 63).astype(F32)
    if key_side:
        f = jnp.where(kind == 2, 64.0 * hi, jnp.where(kind == 3, lo, 0.0))
        if cs is not None:
            f = f + _alibi_key_constants(lane0, cs)
    else:
        c = jnp.where(piece == 0, cs[0], jnp.where(piece == 1, cs[1], cs[2]))
        f = jnp.where(kind == 0, hi, jnp.where(kind == 1, lo, c))
    return jnp.where(live, f, 0.0)


def _alibi_key_constants(lane0, cs):
    kind, piece, live = _alibi_lanes(lane0)
    c = jnp.where(piece == 0, cs[0], jnp.where(piece == 1, cs[1], cs[2]))
    return jnp.where(live, jnp.where(kind == 0, -64.0 * c, jnp.where(kind == 1, -c, 0.0)), 0.0)


def _diff_attn_kernel(cs_ref, lam_ref, q_ref, k_ref, v_ref, subln_ref, o_ref, kaug_ref, vaug_ref, s_ref,
                      pos_ref, *, tq, out_scale):
    h = pl.program_id(1)
    qi = pl.program_id(2)
    E = HEAD_DIM
    S = k_ref.shape[1]
    nb = S // tq
    cs = [cs_ref[ALIBI_SPLIT * h + m] for m in range(ALIBI_SPLIT)]
    lane = lax.broadcasted_iota(jnp.int32, (1, 2 * E), 1)
    first = lane < E

    @pl.when((pl.program_id(0) == 0) & (h == 0) & (qi == 0))
    def _():
        pos = lax.broadcasted_iota(jnp.int32, (S, 1), 0)
        for c, lane0 in enumerate((E, 0)):
            pos_ref[c] = _alibi_features(pos, lane0, None, key_side=True).astype(BF16)

    @pl.when(qi == 0)
    def _():
        k = k_ref[0]
        for c, lane0 in enumerate((E, 0)):
            own = first if c == 0 else jnp.logical_not(first)
            _, _, live = _alibi_lanes(lane0)
            const = _alibi_key_constants(lane0, cs).astype(BF16)
            feat = jnp.where(live, pos_ref[c] + const, jnp.zeros((), BF16))
            kaug_ref[c, 0] = jnp.where(own, k, feat)
            kaug_ref[c, 1] = jnp.where(own, k, -feat)
        ones = jnp.where(lane == 0, 1.0, 0.0).astype(BF16)
        vaug_ref[...] = jnp.concatenate([v_ref[0], jnp.broadcast_to(ones, (S, 2 * E))], 1)

    start = pl.multiple_of(qi * tq, tq)
    half = tq // 2

    def block_max(t):
        bm = t[:, :2 * E]
        for g in range(1, tq // (2 * E)):
            bm = jnp.maximum(bm, t[:, g * 2 * E:(g + 1) * 2 * E])
        return bm

    def softmax_map(c, r0):
        rows = slice(r0, r0 + half)
        own = first if c == 0 else jnp.logical_not(first)
        pos_q = qi * tq + r0 + lax.broadcasted_iota(jnp.int32, (half, 1), 0)
        q = q_ref[0, rows, :]
        qa = jnp.where(own, q, _alibi_features(pos_q, E if c == 0 else 0, cs, key_side=False).astype(BF16))
        lane_max = jnp.full((half, 2 * E), NEG_INF, F32)
        for jb in range(nb):
            side = (jb > qi).astype(jnp.int32)
            t = _dot_nt(qa, kaug_ref[c, side, jb * tq:(jb + 1) * tq, :])
            s_ref[c, jb, rows, :] = t
            lane_max = jnp.maximum(lane_max, jnp.where(jb == qi, NEG_INF, block_max(t)))
            yield
        t = jnp.minimum(s_ref[c, qi, rows, :], _dot_nt(qa, kaug_ref[c, 1, pl.ds(start, tq), :]))
        s_ref[c, qi, rows, :] = t
        m = jnp.max(jnp.maximum(lane_max, block_max(t)), -1, keepdims=True)
        yield
        ps = []
        for jb in range(nb):
            ps.append(jnp.exp2(s_ref[c, jb, rows, :] - m).astype(BF16))
            yield
        ol = jnp.zeros((half, 4 * E), F32)
        for jb in range(nb):
            ol = ol + _dot(ps[jb], vaug_ref[jb * tq:(jb + 1) * tq, :])
            yield
        return ol[:, :2 * E] * (1.0 / ol[:, 2 * E:2 * E + 1])

    items = [(c, r0) for r0 in (0, half) for c in (0, 1)]
    results = _run_pipelined([softmax_map(c, r0) for c, r0 in items], skew=nb + 1)
    for n, r0 in enumerate((0, half)):
        o = results[2 * n] - lam_ref[0] * results[2 * n + 1]
        o = o * lax.rsqrt(jnp.mean(o * o, -1, keepdims=True) + LN_EPS) * subln_ref[...] * out_scale
        o_ref[0, r0:r0 + half, :] = o.astype(o_ref.dtype)


def diff_attention_core(qkv, slope_pieces, lam_full, subln, lambda_init, tq):
    B, S, D3 = qkv.shape
    D = D3 // 3
    H = D // (2 * HEAD_DIM)
    W = 2 * HEAD_DIM
    smem = pl.BlockSpec(memory_space=pltpu.SMEM)
    return pl.pallas_call(
        functools.partial(_diff_attn_kernel, tq=tq, out_scale=1.0 - lambda_init),
        grid=(B, H, S // tq),
        in_specs=[smem, smem,
                  pl.BlockSpec((1, tq, W), lambda b, h, i: (b, i, h)),
                  pl.BlockSpec((1, S, W), lambda b, h, i: (b, 0, H + h)),
                  pl.BlockSpec((1, S, W), lambda b, h, i: (b, 0, 2 * H + h)),
                  pl.BlockSpec((1, W), lambda b, h, i: (0, 0))],
        out_specs=pl.BlockSpec((1, tq, W), lambda b, h, i: (b, i, h)),
        out_shape=jax.ShapeDtypeStruct((B, S, D), BF16),
        scratch_shapes=[pltpu.VMEM((2, 2, S, W), BF16), pltpu.VMEM((S, 2 * W), BF16),
                        pltpu.VMEM((2, S // tq, tq, tq), F32), pltpu.VMEM((2, S, W), BF16)],
        compiler_params=_params("arbitrary", "arbitrary", "arbitrary"),
        name="diff_attention",
    )(slope_pieces, lam_full, qkv, qkv, qkv, subln.reshape(1, W))


def _split_slopes(slopes):
    pieces, rest = [], slopes
    for _ in range(ALIBI_SPLIT):
        piece = rest.astype(BF16).astype(F32)
        pieces.append(piece)
        rest = rest - piece
    return jnp.stack(pieces, -1).reshape(-1)


DIL_STEP_TOKENS = 2048


def _dil_attn_kernel(slope_ref, q_ref, k_ref, v_ref, o_ref, lse_ref, *, tq, half, dilation, nq):
    hp = pl.program_id(1)
    blk = pl.program_id(2)
    E = HEAD_DIM
    L = k_ref.shape[1] // dilation
    kw = tq + 2 * half
    lane = lax.broadcasted_iota(jnp.int32, (tq, 2 * E), 1)
    first = lane < E
    zero = jnp.zeros((tq, 2 * E), BF16)
    slope0 = slope_ref[2 * hp]
    slope1 = slope_ref[2 * hp + 1]
    row = lax.broadcasted_iota(jnp.int32, (tq, 1), 0)
    colk = lax.broadcasted_iota(jnp.int32, (1, kw), 1)

    def rows(start, size):
        return pl.ds(start, size) if dilation == 1 else pl.ds(start, size, stride=dilation)

    def tile(j, r):
        uq = (blk * nq + j) * tq
        ks = jnp.clip(uq - half, 0, L - kw)
        local = rows(j * tq * dilation + r, tq)
        window = rows(ks * dilation + r, kw)
        q = (q_ref[0, local, :] * (E ** -0.5)).astype(BF16)
        k = k_ref[0, window, :].astype(BF16)
        scores = [_dot_nt(jnp.where(first, q, zero), k), _dot_nt(jnp.where(first, zero, q), k)]
        yield
        rel = jnp.abs((ks + colk) - (uq + row))
        dist = (dilation * rel).astype(F32)
        probs, invs, lses = [], [], []
        for s, slope in zip(scores, (slope0, slope1)):
            s = jnp.where(rel <= half, s - slope * dist, NEG_INF)
            m = jnp.max(s, -1, keepdims=True)
            p = jnp.exp(s - m)
            l = jnp.sum(p, -1, keepdims=True)
            probs.append(p.astype(BF16))
            invs.append(1.0 / l)
            lses.append(m + jnp.log(l))
        yield
        v = v_ref[0, window, :].astype(BF16)
        o0 = _dot(probs[0], v) * invs[0]
        o1 = _dot(probs[1], v) * invs[1]
        o_ref[0, local, :] = jnp.where(first, o0, o1)
        lse_ref[0, local, :] = jnp.where(first, lses[0], lses[1])

    _run_pipelined([tile(j, r) for j in range(nq) for r in range(dilation)], skew=1)


def dilated_group_core(qkv, slopes, g, window, dilation, tq):
    B, S, C = qkv.shape
    G = len(DIL_PATTERNS)
    D = C // (3 * G)
    W = 2 * HEAD_DIM
    HP = D // W
    half = window // (2 * dilation)
    base = g * 3 * HP
    nq = DIL_STEP_TOKENS // (tq * dilation)
    smem = pl.BlockSpec(memory_space=pltpu.SMEM)
    out_spec = pl.BlockSpec((1, DIL_STEP_TOKENS, W), lambda b, hp, i: (b, i, hp))
    o, lse = pl.pallas_call(
        functools.partial(_dil_attn_kernel, tq=tq, half=half, dilation=dilation, nq=nq),
        grid=(B, HP, S // DIL_STEP_TOKENS),
        in_specs=[smem,
                  pl.BlockSpec((1, DIL_STEP_TOKENS, W), lambda b, hp, i: (b, i, base + hp)),
                  pl.BlockSpec((1, S, W), lambda b, hp, i: (b, 0, base + HP + hp)),
                  pl.BlockSpec((1, S, W), lambda b, hp, i: (b, 0, base + 2 * HP + hp))],
        out_specs=[out_spec, out_spec],
        out_shape=[jax.ShapeDtypeStruct((B, S, D), F32)] * 2,
        compiler_params=_params("parallel", "parallel", "arbitrary"),
        name=f"dilated_attention_d{dilation}",
    )(slopes, qkv, qkv, qkv)
    return o.reshape(B * S, D), lse.reshape(B * S, D)


def _dil_merge_kernel(o0, o1, o2, l0, l1, l2, out_ref):
    a, b, c = l0[...], l1[...], l2[...]
    m = jnp.maximum(jnp.maximum(a, b), c)
    ea, eb, ec = jnp.exp(a - m), jnp.exp(b - m), jnp.exp(c - m)
    inv = 1.0 / (ea + eb + ec)
    out_ref[...] = ((ea * inv) * o0[...] + (eb * inv) * o1[...] + (ec * inv) * o2[...]).astype(out_ref.dtype)


def dilated_merge(outs, lses, tm):
    M, D = outs[0].shape
    spec = pl.BlockSpec((tm, D), lambda i: (i, 0))
    return pl.pallas_call(
        _dil_merge_kernel,
        grid=(M // tm,),
        in_specs=[spec] * 6,
        out_specs=spec,
        out_shape=jax.ShapeDtypeStruct((M, D), BF16),
        compiler_params=_params("parallel"),
        name="dilated_merge",
    )(*outs, *lses)


def _shifted_rows(x, prev_ref, next_ref, i, n_i, reverse):
    tm = x.shape[0]
    r = lax.broadcasted_iota(jnp.int32, (tm, 1), 0)
    if reverse:
        edge = jnp.where(i == n_i - 1, 0.0, next_ref[0, 0:1, :])
        return jnp.where(r == tm - 1, edge, pltpu.roll(x, tm - 1, 0))
    edge = jnp.where(i == 0, 0.0, prev_ref[0, 7:8, :])
    return jnp.where(r == 0, edge, pltpu.roll(x, 1, 0))


def _rwkv_proj_kernel(x_ref, prev_ref, next_ref, mu_ref, wr_ref, wk_ref, wv_ref, w0_ref, w1_ref, w2_ref,
                      a0_ref, a1_ref, a2_ref, g1_ref, g2_ref,
                      r_ref, k_ref, v_ref, lw_ref, a_ref, g_ref, *, reverse):
    i = pl.program_id(1)
    x = x_ref[0]
    xx = _shifted_rows(x, prev_ref, next_ref, i, pl.num_programs(1), reverse) - x

    def projection(j, w_in, finish, out_ref):
        xm = (x + xx * mu_ref[j:j + 1, :]).astype(BF16)
        yield
        y = _dot(xm, w_in[...])
        yield
        out_ref[0] = finish(y).astype(out_ref.dtype)

    def decay(y):
        z = -(w0_ref[...] + _dot(jnp.tanh(y).astype(BF16), w2_ref[...]))
        softplus = jnp.maximum(z, 0.0) + jnp.log(1.0 + jnp.exp(-jnp.abs(z)))
        return -jnp.exp(-softplus - 0.5)

    def rate(y):
        return jax.nn.sigmoid(a0_ref[...] + _dot(y.astype(BF16), a2_ref[...]))

    def gate(y):
        return _dot(jax.nn.sigmoid(y).astype(BF16), g2_ref[...])

    same = lambda y: y
    _run_pipelined([projection(0, wr_ref, same, r_ref), projection(1, w1_ref, decay, lw_ref),
                    projection(2, wk_ref, same, k_ref), projection(4, a1_ref, rate, a_ref),
                    projection(3, wv_ref, same, v_ref), projection(5, g1_ref, gate, g_ref)], skew=1)


def rwkv_projections(x, mu, w_rkv, w0, w1, w2, a0, a1, a2, g1, g2, reverse, tm):
    B, S, D = x.shape
    nb = tm // 8
    fixed2 = lambda b, i: (0, 0)
    tile = pl.BlockSpec((1, tm, D), lambda b, i: (b, i, 0))
    full = lambda a: pl.BlockSpec(a.shape, fixed2)
    ops = (mu, w_rkv[0], w_rkv[1], w_rkv[2], w0.reshape(1, D), w1, w2, a0.reshape(1, D), a1, a2, g1, g2)
    return pl.pallas_call(
        functools.partial(_rwkv_proj_kernel, reverse=reverse),
        grid=(B, S // tm),
        in_specs=[tile,
                  pl.BlockSpec((1, 8, D), lambda b, i: (b, jnp.maximum(i * nb - 1, 0), 0)),
                  pl.BlockSpec((1, 8, D), lambda b, i: (b, jnp.minimum((i + 1) * nb, S // 8 - 1), 0))]
        + [full(a) for a in ops],
        out_specs=[tile] * 6,
        out_shape=[jax.ShapeDtypeStruct((B, S, D), dt) for dt in (BF16, BF16, BF16, F32, F32, BF16)],
        compiler_params=_params("parallel", "arbitrary"),
        name="rwkv_projections_bwd" if reverse else "rwkv_projections_fwd",
    )(x, x, x, *ops)


SLAB_HEADS = 4
SLAB = SLAB_HEADS * HEAD_DIM


def _split2(x):
    hi = x.astype(BF16)
    lo = (x - hi.astype(F32)).astype(BF16)
    return hi, lo


def _block_diag(x, block_mask):
    xb = x.astype(BF16)
    return jnp.where(block_mask, jnp.concatenate([xb] * SLAB_HEADS, 0), jnp.zeros((), BF16))


def _per_head_dot(lhs, x, block_mask, nt=False):
    mm = _dot_nt if nt else _dot
    return mm(lhs.astype(BF16), _block_diag(x, block_mask))


def _per_head_sums(xs, ones_bd):
    M = xs[0].shape[0]
    n = len(xs) * M
    s = _dot(jnp.concatenate(_split2(jnp.concatenate(xs, 0)), 0), ones_bd)
    tot = s[:n] + s[n:]
    return [tot[i * M:(i + 1) * M] for i in range(len(xs))]


def _wkv_slab_chunk(state, r, k, v, cum, lw, a, kkw, kaw, rkw, gng, gnb, gate, reverse, consts):
    block_mask, ones_bd, strict, incl, eye, lane_head = consts
    C = r.shape[0]
    kk = k * kkw
    k = k * (1.0 + (a - 1.0) * kaw)
    sums = yield jnp.concatenate([kk * kk, r * k * rkw], 0)
    kk = kk * lax.rsqrt(jnp.maximum(sums[:C], 1e-24))
    bonus = sums[C:]
    dec = jnp.exp(cum)
    inv = jnp.exp(-cum)
    alpha_bar = -kk * jnp.exp(cum - lw)
    beta_t = kk * a * inv
    k_t = k * inv
    r_bar = r * dec

    ar = jnp.concatenate([alpha_bar, r_bar], 0)
    gb = _per_head_dot(ar, beta_t, block_mask, nt=True)
    gk = _per_head_dot(ar, k_t, block_mask, nt=True)
    yield
    a_ab = jnp.where(strict, gb[:C], 0.0)
    a_rb = jnp.where(incl, gb[C:], 0.0)
    a_ak = jnp.where(strict, gk[:C], 0.0)
    a_rk = jnp.where(incl, gk[C:], 0.0)

    x = eye + a_ab
    p = _per_head_dot(a_ab, a_ab, block_mask)
    av = _per_head_dot(jnp.concatenate([a_ak, a_rk], 0), v, block_mask)
    ss = _per_head_dot(ar, state, block_mask, nt=True)
    yield
    m = 2
    while 2 * m < C:
        xp = _per_head_dot(jnp.concatenate([x, p], 0), p, block_mask)
        yield
        x = x + xp[:C]
        p = xp[C:]
        m *= 2
    xp = _per_head_dot(x, p, block_mask)
    yield
    x = x + xp
    u = _per_head_dot(x, ss[:C] + av[:C], block_mask)
    yield
    y = ss[C:] + _per_head_dot(a_rb, u, block_mask) + av[C:]

    total = cum[0:1] if reverse else cum[C - 1:C]
    end = jnp.exp(total)
    cross = lax.dot_general(jnp.concatenate([u, v], 0).astype(BF16),
                            jnp.concatenate([beta_t * end, k_t * end], 0).astype(BF16),
                            TN_DIMS, preferred_element_type=F32)
    mu = (yield y) * (1.0 / HEAD_DIM)
    new_state = state * end
    for hh in range(SLAB_HEADS):
        new_state = new_state + jnp.where(lane_head == hh, cross[hh * HEAD_DIM:(hh + 1) * HEAD_DIM], 0.0)

    yc = y - mu
    var = (yield yc * yc) * (1.0 / HEAD_DIM)
    out = yc * lax.rsqrt(var + GN_EPS) * gng + gnb + bonus * v
    return new_state, out * gate


def _wkv_kernel(rf, kf, vf, lwf, af, gf, rb, kb, vb, lwb, ab, gb,
                kk_ref, ka_ref, rk_ref, gng_ref, gnb_ref, of_ref, ob_ref, state_ref):
    c = pl.program_id(1)
    C = rf.shape[1]
    D = rf.shape[2]
    N = HEAD_DIM

    @pl.when(c == 0)
    def _():
        state_ref[...] = jnp.zeros_like(state_ref)

    row = lax.broadcasted_iota(jnp.int32, (SLAB, SLAB), 0)
    col = lax.broadcasted_iota(jnp.int32, (SLAB, SLAB), 1)
    block_mask = (row // N) == (col // N)
    ones_bd = jnp.where(block_mask, 1.0, 0.0).astype(BF16)
    t = lax.broadcasted_iota(jnp.int32, (C, SLAB), 0)
    lane = lax.broadcasted_iota(jnp.int32, (C, SLAB), 1)
    s = lane % C
    lane_head = lax.broadcasted_iota(jnp.int32, (N, SLAB), 1) // N
    ti = lax.broadcasted_iota(jnp.int32, (C, C), 0)
    si = lax.broadcasted_iota(jnp.int32, (C, C), 1)
    eye = (s == t).astype(F32)

    chains = []
    for d, (refs, o_ref) in enumerate((((rf, kf, vf, lwf, af, gf), of_ref), ((rb, kb, vb, lwb, ab, gb), ob_ref))):
        reverse = d == 1
        strict = (s > t) if reverse else (s < t)
        incl = (s >= t) if reverse else (s <= t)
        consts = (block_mask, ones_bd, strict, incl, eye, lane_head)
        tri = ((si >= ti) if reverse else (si <= ti)).astype(BF16)
        lw_all = refs[3][0]
        parts = _dot(tri, jnp.concatenate(_split2(lw_all), 1))
        cum_all = parts[:, :D] + parts[:, D:]
        for sb in range(D // SLAB):
            sl = slice(sb * SLAB, (sb + 1) * SLAB)
            r, k, v, lw, a, gate = (ref[0, :, sl].astype(F32) for ref in refs)
            chains.append((d, sb, sl, o_ref, _wkv_slab_chunk(
                state_ref[d, sb], r, k, v, cum_all[:, sl], lw, a,
                kk_ref[d:d + 1, sl], ka_ref[d:d + 1, sl], rk_ref[:, sl], gng_ref[:, sl], gnb_ref[:, sl],
                gate, reverse, consts)))

    slabs = [chain for *_, chain in chains]
    asks = [next(chain) for chain in slabs]
    results = [None] * len(slabs)
    while any(r is None for r in results):
        replies = [None] * len(slabs) if asks[0] is None else _per_head_sums(asks, ones_bd)
        for n, chain in enumerate(slabs):
            try:
                asks[n] = chain.send(replies[n])
            except StopIteration as stop:
                results[n] = stop.value
    for (d, sb, sl, o_ref, _), (new_state, out) in zip(chains, results):
        state_ref[d, sb] = new_state
        o_ref[0, :, sl] = out


def rwkv_scan(fwd, bwd, k_k, k_a, r_k, gn_g, gn_b):
    B, S, D = fwd[0].shape
    C = WKV_CHUNK
    NC = S // C
    f_spec = pl.BlockSpec((1, C, D), lambda b, c: (b, c, 0))
    b_spec = pl.BlockSpec((1, C, D), lambda b, c: (b, NC - 1 - c, 0))
    two = pl.BlockSpec((2, D), lambda b, c: (0, 0))
    one = pl.BlockSpec((1, D), lambda b, c: (0, 0))
    return pl.pallas_call(
        _wkv_kernel,
        grid=(B, NC),
        in_specs=[f_spec] * 6 + [b_spec] * 6 + [two, two, one, one, one],
        out_specs=[f_spec, b_spec],
        out_shape=[jax.ShapeDtypeStruct((B, S, D), F32)] * 2,
        scratch_shapes=[pltpu.VMEM((2, D // SLAB, HEAD_DIM, SLAB), F32)],
        compiler_params=_params("parallel", "arbitrary"),
        name="rwkv_scan",
    )(*fwd, *bwd, k_k, k_a, r_k.reshape(1, D), gn_g.reshape(1, D), gn_b.reshape(1, D))


def _xatt_kernel(x_ref, kv_ref, wq_ref, wo_ref, g_ref, b_ref, o_ref, *, alpha):
    D = x_ref.shape[-1]
    E = D // XATT_HEADS

    def row_chunk(r0):
        rows = slice(r0, r0 + ROW_CHUNK)
        q = (_dot(x_ref[0, rows, :].astype(BF16), wq_ref[...]) * (E ** -0.5)).astype(BF16)
        yield
        scores = [_dot_nt(q[:, h * E:(h + 1) * E], kv_ref[0, :, h * E:(h + 1) * E]) for h in range(XATT_HEADS)]
        yield
        probs, invs = [], []
        for s in scores:
            p = jnp.exp(s - jnp.max(s, -1, keepdims=True))
            invs.append(1.0 / jnp.sum(p, -1, keepdims=True))
            probs.append(p.astype(BF16))
        yield
        outs = [(_dot(probs[h], kv_ref[0, :, D + h * E:D + (h + 1) * E]) * invs[h]).astype(BF16)
                for h in range(XATT_HEADS)]
        yield
        y = _dot(jnp.concatenate(outs, -1), wo_ref[...])
        yield
        o_ref[0, rows, :] = _layer_norm(alpha * x_ref[0, rows, :] + y, g_ref[...], b_ref[...])

    _run_pipelined([row_chunk(r0) for r0 in range(0, o_ref.shape[1], ROW_CHUNK)], skew=1)


def cross_attention_sublayer(x, kv, w_q, w_o, g, b, alpha, tm):
    B, S, D = x.shape
    M = kv.shape[1]
    fixed = lambda bb, i: (0, 0)
    tile = pl.BlockSpec((1, tm, D), lambda bb, i: (bb, i, 0))
    return pl.pallas_call(
        functools.partial(_xatt_kernel, alpha=alpha),
        grid=(B, S // tm),
        in_specs=[tile, pl.BlockSpec((1, M, 2 * D), lambda bb, i: (bb, 0, 0)),
                  pl.BlockSpec((D, D), fixed), pl.BlockSpec((D, D), fixed),
                  pl.BlockSpec((1, D), fixed), pl.BlockSpec((1, D), fixed)],
        out_specs=tile,
        out_shape=jax.ShapeDtypeStruct((B, S, D), F32),
        compiler_params=_params("parallel", "arbitrary"),
        name="cross_attention",
    )(x, kv, w_q, w_o, g.reshape(1, D), b.reshape(1, D))


FFN_LANES = 128


def _ffn_in_kernel(x_ref, prev_ref, next_ref, wg_ref, wv_ref, cw_ref, cb_ref, o_ref, w_ref):
    i = pl.program_id(2)
    n_i = pl.num_programs(2)
    tm = x_ref.shape[1]
    C = FFN_LANES
    n_chunks = o_ref.shape[2] // C

    @pl.when((pl.program_id(1) == 0) & (i == 0))
    def _():
        for n in range(n_chunks):
            w_ref[:, 2 * C * n:2 * C * n + C] = wg_ref[:, C * n:C * (n + 1)]
            w_ref[:, 2 * C * n + C:2 * C * (n + 1)] = wv_ref[:, C * n:C * (n + 1)]

    x = jnp.concatenate([prev_ref[0], x_ref[0], next_ref[0]], 0).astype(BF16)
    r = lax.broadcasted_iota(jnp.int32, (tm, 1), 0)

    def column_chunk(n):
        y = _dot(x, w_ref[:, 2 * C * n:2 * C * (n + 1)])
        yield
        gate = y[8:8 + tm, :C]
        val = y[8:8 + tm, C:]
        g_prev = jnp.where(i == 0, 0.0, y[7:8, :C])
        g_next = jnp.where(i == n_i - 1, 0.0, y[tm + 8:tm + 9, :C])
        before = jnp.where(r == 0, g_prev, pltpu.roll(gate, 1, 0))
        after = jnp.where(r == tm - 1, g_next, pltpu.roll(gate, tm - 1, 0))
        cols = slice(C * n, C * (n + 1))
        conv = before * cw_ref[0:1, cols] + gate * cw_ref[1:2, cols] + after * cw_ref[2:3, cols] + cb_ref[:, cols]
        gelu = 0.5 * conv * (1.0 + lax.erf(conv * (2.0 ** -0.5)))
        o_ref[0, :, cols] = (gelu * val).astype(o_ref.dtype)

    _run_pipelined([column_chunk(n) for n in range(n_chunks)], skew=1)


def ffn_in(x, w_in, conv_w, conv_b, tm, tn):
    B, S, D = x.shape
    F = w_in.shape[1] // 2
    nb = tm // 8
    nj = F // tn
    return pl.pallas_call(
        _ffn_in_kernel,
        grid=(nj, B, S // tm),
        in_specs=[pl.BlockSpec((1, tm, D), lambda j, b, i: (b, i, 0)),
                  pl.BlockSpec((1, 8, D), lambda j, b, i: (b, jnp.maximum(i * nb - 1, 0), 0)),
                  pl.BlockSpec((1, 8, D), lambda j, b, i: (b, jnp.minimum((i + 1) * nb, S // 8 - 1), 0)),
                  pl.BlockSpec((D, tn), lambda j, b, i: (0, j)),
                  pl.BlockSpec((D, tn), lambda j, b, i: (0, nj + j)),
                  pl.BlockSpec((3, tn), lambda j, b, i: (0, j)),
                  pl.BlockSpec((1, tn), lambda j, b, i: (0, j))],
        out_specs=pl.BlockSpec((1, tm, tn), lambda j, b, i: (b, i, j)),
        out_shape=jax.ShapeDtypeStruct((B, S, F), BF16),
        scratch_shapes=[pltpu.VMEM((D, 2 * tn), BF16)],
        compiler_params=_params("arbitrary", "arbitrary", "arbitrary"),
        name="ffn_in",
    )(x, x, x, w_in, w_in, conv_w, conv_b.reshape(1, F))


def _alibi_slopes(n):
    return jnp.exp2(-8.0 * jnp.arange(1, n + 1, dtype=F32) / n)


def kernel(x, mem, diff_w_qkv, diff_lambda, diff_subln, diff_w_o, dil_w_qkv, dil_w_o, rwkv_mu, rwkv_w_rkv, rwkv_w0, rwkv_w1, rwkv_w2, rwkv_a0, rwkv_a1, rwkv_a2, rwkv_g1, rwkv_g2, rwkv_k_k, rwkv_k_a, rwkv_r_k, rwkv_gn_g, rwkv_gn_b, rwkv_w_o, xatt_w_q, xatt_w_kv, xatt_w_o, ffn_w_in, ffn_conv_w, ffn_conv_b, ffn_w_out, ln_g, ln_b):
    B, S, D = x.shape
    depth = xatt_w_q.shape[0]
    alpha = (2 * depth) ** 0.25
    T = B * S
    bf = lambda a: a.astype(BF16)
    mem2 = mem.reshape(-1, D)

    for i in range(depth):
        m, j = i % N_MIXERS, i // N_MIXERS
        x2 = x.reshape(T, D)
        if m == 0:
            lambda_init = 0.8 - 0.6 * math.exp(-0.3 * i)
            lamf = diff_lambda[j].astype(F32)
            lam_full = (jnp.exp(jnp.sum(lamf[0] * lamf[1])) - jnp.exp(jnp.sum(lamf[2] * lamf[3]))
                        + lambda_init).reshape(1)
            q_scale = jnp.where(jnp.arange(3 * D) < D, LOG2E * HEAD_DIM ** -0.5, 1.0).astype(F32)
            qkv = matmul(x2, bf(diff_w_qkv[j]), BF16, tm=1024, tn=3 * D, col_scale=q_scale).reshape(B, S, 3 * D)
            o = diff_attention_core(qkv, _split_slopes(LOG2E * _alibi_slopes(D // (2 * HEAD_DIM))), lam_full,
                                    diff_subln[j], lambda_init, tq=512)
            hs, w_o = [o.reshape(T, D)], diff_w_o[j]
        elif m == 1:
            qkv = matmul(x2, bf(dil_w_qkv[j]), F32, tm=512, tn=3 * D).reshape(B, S, -1)
            slopes = _alibi_slopes(D // HEAD_DIM)
            parts = [dilated_group_core(qkv, slopes, g, window, dilation, tq=128)
                     for g, (window, dilation) in enumerate(DIL_PATTERNS)]
            hs = [dilated_merge([p[0] for p in parts], [p[1] for p in parts], tm=512)]
            w_o = dil_w_o[j]
        else:
            dirs = [rwkv_projections(x, rwkv_mu[j, d], bf(rwkv_w_rkv[j]), rwkv_w0[j, d], bf(rwkv_w1[j, d]),
                                     bf(rwkv_w2[j, d]), rwkv_a0[j, d], bf(rwkv_a1[j, d]), bf(rwkv_a2[j, d]),
                                     bf(rwkv_g1[j, d]), bf(rwkv_g2[j, d]), reverse=d == 1, tm=512)
                    for d in range(2)]
            yf, yb = rwkv_scan(dirs[0], dirs[1], rwkv_k_k[j], rwkv_k_a[j], rwkv_r_k[j],
                               rwkv_gn_g[j], rwkv_gn_b[j])
            hs, w_o = [yf.reshape(T, D), yb.reshape(T, D)], rwkv_w_o[j]
        x2 = matmul_residual_ln(hs, bf(w_o), x2, ln_g[i, 0], ln_b[i, 0], alpha, tm=1024)

        kv = matmul(mem2, bf(xatt_w_kv[i]), BF16, tm=512, tn=2 * D).reshape(B, -1, 2 * D)
        x = cross_attention_sublayer(x2.reshape(B, S, D), kv, bf(xatt_w_q[i]), bf(xatt_w_o[i]),
                                     ln_g[i, 1], ln_b[i, 1], alpha, tm=1024)

        act = ffn_in(x, bf(ffn_w_in[i]), ffn_conv_w[i], ffn_conv_b[i], tm=512, tn=1408)
        x = matmul_residual_ln([act.reshape(T, -1)], bf(ffn_w_out[i]), x.reshape(T, D),
                               ln_g[i, 2], ln_b[i, 2], alpha, tm=1024).reshape(B, S, D)
    return x
```

```python
import functools
import math

import jax
import jax.numpy as jnp
from jax import lax
from jax.experimental import pallas as pl
from jax.experimental.pallas import tpu as pltpu

BF16 = jnp.bfloat16
F32 = jnp.float32

HEAD_DIM = 64
N_MIXERS = 3
DIL_PATTERNS = ((128, 1), (512, 4), (2048, 16))
XATT_HEADS = 4
LN_EPS = 1e-5
GN_EPS = 64e-5
NEG_INF = -1e30
WKV_CHUNK = 64
VMEM_LIMIT_BYTES = 56 * 1024 * 1024
LOG2E = math.log2(math.e)

NT_DIMS = (((1,), (1,)), ((), ()))
TN_DIMS = (((0,), (0,)), ((), ()))


def _params(*semantics):
    return pltpu.CompilerParams(dimension_semantics=semantics, vmem_limit_bytes=VMEM_LIMIT_BYTES)


def _dot(a, b):
    return jnp.dot(a, b, preferred_element_type=F32)


def _dot_nt(a, b, precision=None):
    return lax.dot_general(a, b, NT_DIMS, preferred_element_type=F32, precision=precision)


def _layer_norm(y, g, b):
    mu = jnp.mean(y, -1, keepdims=True)
    yc = y - mu
    var = jnp.mean(yc * yc, -1, keepdims=True)
    return yc * lax.rsqrt(var + LN_EPS) * g + b


def _run_pipelined(chains, skew):
    results = [None] * len(chains)
    live = [True] * len(chains)
    tick = 0
    while any(live):
        for n, chain in enumerate(chains):
            if live[n] and tick >= n * skew:
                try:
                    next(chain)
                except StopIteration as stop:
                    results[n] = stop.value
                    live[n] = False
        tick += 1
    return results


MM_COL_CHUNK = 512


def _mm_kernel(x_ref, w_ref, *rest):
    o_ref = rest[-1]
    x = x_ref[...].astype(BF16)
    for c0 in range(0, o_ref.shape[1], MM_COL_CHUNK):
        cols = slice(c0, c0 + MM_COL_CHUNK)
        y = _dot(x, w_ref[:, cols])
        if len(rest) == 2:
            y = y * rest[0][:, cols]
        o_ref[:, cols] = y.astype(o_ref.dtype)


def matmul(x, w, out_dtype, tm, tn, col_scale=None):
    M, K = x.shape
    N = w.shape[1]
    in_specs = [pl.BlockSpec((tm, K), lambda j, i: (i, 0)), pl.BlockSpec((K, tn), lambda j, i: (0, j))]
    operands = [x, w]
    if col_scale is not None:
        in_specs.append(pl.BlockSpec((1, tn), lambda j, i: (0, j)))
        operands.append(col_scale.reshape(1, N))
    return pl.pallas_call(
        _mm_kernel,
        grid=(N // tn, M // tm),
        in_specs=in_specs,
        out_specs=pl.BlockSpec((tm, tn), lambda j, i: (i, j)),
        out_shape=jax.ShapeDtypeStruct((M, N), out_dtype),
        compiler_params=_params("arbitrary", "arbitrary"),
        name="matmul",
    )(*operands)


ROW_CHUNK = 256


def _mm_res_ln_kernel(*refs, n_h, alpha):
    h_refs = refs[:n_h]
    w_ref, x_ref, g_ref, b_ref, o_ref = refs[n_h:]

    def row_chunk(r0):
        rows = slice(r0, r0 + ROW_CHUNK)
        h = h_refs[0][rows, :]
        for r in h_refs[1:]:
            h = h + r[rows, :]
        hw = _dot(h.astype(BF16), w_ref[...])
        yield
        o_ref[rows, :] = _layer_norm(alpha * x_ref[rows, :] + hw, g_ref[...], b_ref[...])

    _run_pipelined([row_chunk(r0) for r0 in range(0, o_ref.shape[0], ROW_CHUNK)], skew=1)


def matmul_residual_ln(hs, w, x, g, b, alpha, tm):
    M, K = hs[0].shape
    D = w.shape[1]
    row = lambda i: (i, 0)
    fixed = lambda i: (0, 0)
    return pl.pallas_call(
        functools.partial(_mm_res_ln_kernel, n_h=len(hs), alpha=alpha),
        grid=(M // tm,),
        in_specs=[pl.BlockSpec((tm, K), row) for _ in hs]
        + [pl.BlockSpec((K, D), fixed), pl.BlockSpec((tm, D), row),
           pl.BlockSpec((1, D), fixed), pl.BlockSpec((1, D), fixed)],
        out_specs=pl.BlockSpec((tm, D), row),
        out_shape=jax.ShapeDtypeStruct((M, D), F32),
        compiler_params=_params("parallel"),
        name="matmul_residual_ln",
    )(*hs, w, x, g.reshape(1, D), b.reshape(1, D))


ALIBI_SPLIT = 3
ALIBI_FEATS = 4 * ALIBI_SPLIT


def _alibi_lanes(lane0):
    lane = lax.broadcasted_iota(jnp.int32, (1, 2 * HEAD_DIM), 1) - lane0
    return lane & 3, lane >> 2, (lane >= 0) & (lane < ALIBI_FEATS)


def _alibi_features(pos, lane0, cs, key_side):
    kind, piece, live = _alibi_lanes(lane0)
    hi = (pos >> 6).astype(F32)
    lo = (pos & 63).astype(F32)
    if key_side:
        f = jnp.where(kind == 2, 64.0 * hi, jnp.where(kind == 3, lo, 0.0))
        if cs is not None:
            f = f + _alibi_key_constants(lane0, cs)
    else:
        c = jnp.where(piece == 0, cs[0], jnp.where(piece == 1, cs[1], cs[2]))
        f = jnp.where(kind == 0, hi, jnp.where(kind == 1, lo, c))
    return jnp.where(live, f, 0.0)


def _alibi_key_constants(lane0, cs):
    kind, piece, live = _alibi_lanes(lane0)
    c = jnp.where(piece == 0, cs[0], jnp.where(piece == 1, cs[1], cs[2]))
    return jnp.where(live, jnp.where(kind == 0, -64.0 * c, jnp.where(kind == 1, -c, 0.0)), 0.0)


def _diff_attn_kernel(cs_ref, lam_ref, q_ref, k_ref, v_ref, subln_ref, o_ref, kaug_ref, vaug_ref, s_ref,
                      pos_ref, *, tq, out_scale):
    h = pl.program_id(1)
    qi = pl.program_id(2)
    E = HEAD_DIM
    S = k_ref.shape[1]
    nb = S // tq
    cs = [cs_ref[ALIBI_SPLIT * h + m] for m in range(ALIBI_SPLIT)]
    lane = lax.broadcasted_iota(jnp.int32, (1, 2 * E), 1)
    first = lane < E

    @pl.when((pl.program_id(0) == 0) & (h == 0) & (qi == 0))
    def _():
        pos = lax.broadcasted_iota(jnp.int32, (S, 1), 0)
        for c, lane0 in enumerate((E, 0)):
            pos_ref[c] = _alibi_features(pos, lane0, None, key_side=True).astype(BF16)

    @pl.when(qi == 0)
    def _():
        k = k_ref[0]
        for c, lane0 in enumerate((E, 0)):
            own = first if c == 0 else jnp.logical_not(first)
            _, _, live = _alibi_lanes(lane0)
            const = _alibi_key_constants(lane0, cs).astype(BF16)
            feat = jnp.where(live, pos_ref[c] + const, jnp.zeros((), BF16))
            kaug_ref[c, 0] = jnp.where(own, k, feat)
            kaug_ref[c, 1] = jnp.where(own, k, -feat)
        ones = jnp.where(lane == 0, 1.0, 0.0).astype(BF16)
        vaug_ref[...] = jnp.concatenate([v_ref[0], jnp.broadcast_to(ones, (S, 2 * E))], 1)

    start = pl.multiple_of(qi * tq, tq)
    half = tq // 2

    def block_max(t):
        bm = t[:, :2 * E]
        for g in range(1, tq // (2 * E)):
            bm = jnp.maximum(bm, t[:, g * 2 * E:(g + 1) * 2 * E])
        return bm

    def softmax_map(c, r0):
        rows = slice(r0, r0 + half)
        own = first if c == 0 else jnp.logical_not(first)
        pos_q = qi * tq + r0 + lax.broadcasted_iota(jnp.int32, (half, 1), 0)
        q = q_ref[0, rows, :]
        qa = jnp.where(own, q, _alibi_features(pos_q, E if c == 0 else 0, cs, key_side=False).astype(BF16))
        lane_max = jnp.full((half, 2 * E), NEG_INF, F32)
        for jb in range(nb):
            side = (jb > qi).astype(jnp.int32)
            t = _dot_nt(qa, kaug_ref[c, side, jb * tq:(jb + 1) * tq, :])
            s_ref[c, jb, rows, :] = t
            lane_max = jnp.maximum(lane_max, jnp.where(jb == qi, NEG_INF, block_max(t)))
            yield
        t = jnp.minimum(s_ref[c, qi, rows, :], _dot_nt(qa, kaug_ref[c, 1, pl.ds(start, tq), :]))
        s_ref[c, qi, rows, :] = t
        m = jnp.max(jnp.maximum(lane_max, block_max(t)), -1, keepdims=True)
        yield
        ps = []
        for jb in range(nb):
            ps.append(jnp.exp2(s_ref[c, jb, rows, :] - m).astype(BF16))
            yield
        ol = jnp.zeros((half, 4 * E), F32)
        for jb in range(nb):
            ol = ol + _dot(ps[jb], vaug_ref[jb * tq:(jb + 1) * tq, :])
            yield
        return ol[:, :2 * E] * (1.0 / ol[:, 2 * E:2 * E + 1])

    items = [(c, r0) for r0 in (0, half) for c in (0, 1)]
    results = _run_pipelined([softmax_map(c, r0) for c, r0 in items], skew=nb + 1)
    for n, r0 in enumerate((0, half)):
        o = results[2 * n] - lam_ref[0] * results[2 * n + 1]
        o = o * lax.rsqrt(jnp.mean(o * o, -1, keepdims=True) + LN_EPS) * subln_ref[...] * out_scale
        o_ref[0, r0:r0 + half, :] = o.astype(o_ref.dtype)


def diff_attention_core(qkv, slope_pieces, lam_full, subln, lambda_init, tq):
    B, S, D3 = qkv.shape
    D = D3 // 3
    H = D // (2 * HEAD_DIM)
    W = 2 * HEAD_DIM
    smem = pl.BlockSpec(memory_space=pltpu.SMEM)
    return pl.pallas_call(
        functools.partial(_diff_attn_kernel, tq=tq, out_scale=1.0 - lambda_init),
        grid=(B, H, S // tq),
        in_specs=[smem, smem,
                  pl.BlockSpec((1, tq, W), lambda b, h, i: (b, i, h)),
                  pl.BlockSpec((1, S, W), lambda b, h, i: (b, 0, H + h)),
                  pl.BlockSpec((1, S, W), lambda b, h, i: (b, 0, 2 * H + h)),
                  pl.BlockSpec((1, W), lambda b, h, i: (0, 0))],
        out_specs=pl.BlockSpec((1, tq, W), lambda b, h, i: (b, i, h)),
        out_shape=jax.ShapeDtypeStruct((B, S, D), BF16),
        scratch_shapes=[pltpu.VMEM((2, 2, S, W), BF16), pltpu.VMEM((S, 2 * W), BF16),
                        pltpu.VMEM((2, S // tq, tq, tq), F32), pltpu.VMEM((2, S, W), BF16)],
        compiler_params=_params("arbitrary", "arbitrary", "arbitrary"),
        name="diff_attention",
    )(slope_pieces, lam_full, qkv, qkv, qkv, subln.reshape(1, W))


def _split_slopes(slopes):
    pieces, rest = [], slopes
    for _ in range(ALIBI_SPLIT):
        piece = rest.astype(BF16).astype(F32)
        pieces.append(piece)
        rest = rest - piece
    return jnp.stack(pieces, -1).reshape(-1)


DIL_STEP_TOKENS = 2048


def _dil_attn_kernel(slope_ref, q_ref, k_ref, v_ref, o_ref, lse_ref, *, tq, half, dilation, nq):
    hp = pl.program_id(1)
    blk = pl.program_id(2)
    E = HEAD_DIM
    L = k_ref.shape[1] // dilation
    kw = tq + 2 * half
    lane = lax.broadcasted_iota(jnp.int32, (tq, 2 * E), 1)
    first = lane < E
    zero = jnp.zeros((tq, 2 * E), BF16)
    slope0 = slope_ref[2 * hp]
    slope1 = slope_ref[2 * hp + 1]
    row = lax.broadcasted_iota(jnp.int32, (tq, 1), 0)
    colk = lax.broadcasted_iota(jnp.int32, (1, kw), 1)

    def rows(start, size):
        return pl.ds(start, size) if dilation == 1 else pl.ds(start, size, stride=dilation)

    def tile(j, r):
        uq = (blk * nq + j) * tq
        ks = jnp.clip(uq - half, 0, L - kw)
        local = rows(j * tq * dilation + r, tq)
        window = rows(ks * dilation + r, kw)
        q = (q_ref[0, local, :] * (E ** -0.5)).astype(BF16)
        k = k_ref[0, window, :].astype(BF16)
        scores = [_dot_nt(jnp.where(first, q, zero), k), _dot_nt(jnp.where(first, zero, q), k)]
        yield
        rel = jnp.abs((ks + colk) - (uq + row))
        dist = (dilation * rel).astype(F32)
        probs, invs, lses = [], [], []
        for s, slope in zip(scores, (slope0, slope1)):
            s = jnp.where(rel <= half, s - slope * dist, NEG_INF)
            m = jnp.max(s, -1, keepdims=True)
            p = jnp.exp(s - m)
            l = jnp.sum(p, -1, keepdims=True)
            probs.append(p.astype(BF16))
            invs.append(1.0 / l)
            lses.append(m + jnp.log(l))
        yield
        v = v_ref[0, window, :].astype(BF16)
        o0 = _dot(probs[0], v) * invs[0]
        o1 = _dot(probs[1], v) * invs[1]
        o_ref[0, local, :] = jnp.where(first, o0, o1)
        lse_ref[0, local, :] = jnp.where(first, lses[0], lses[1])

    _run_pipelined([tile(j, r) for j in range(nq) for r in range(dilation)], skew=1)


def dilated_group_core(qkv, slopes, g, window, dilation, tq):
    B, S, C = qkv.shape
    G = len(DIL_PATTERNS)
    D = C // (3 * G)
    W = 2 * HEAD_DIM
    HP = D // W
    half = window // (2 * dilation)
    base = g * 3 * HP
    nq = DIL_STEP_TOKENS // (tq * dilation)
    smem = pl.BlockSpec(memory_space=pltpu.SMEM)
    out_spec = pl.BlockSpec((1, DIL_STEP_TOKENS, W), lambda b, hp, i: (b, i, hp))
    o, lse = pl.pallas_call(
        functools.partial(_dil_attn_kernel, tq=tq, half=half, dilation=dilation, nq=nq),
        grid=(B, HP, S // DIL_STEP_TOKENS),
        in_specs=[smem,
                  pl.BlockSpec((1, DIL_STEP_TOKENS, W), lambda b, hp, i: (b, i, base + hp)),
                  pl.BlockSpec((1, S, W), lambda b, hp, i: (b, 0, base + HP + hp)),
                  pl.BlockSpec((1, S, W), lambda b, hp, i: (b, 0, base + 2 * HP + hp))],
        out_specs=[out_spec, out_spec],
        out_shape=[jax.ShapeDtypeStruct((B, S, D), F32)] * 2,
        compiler_params=_params("parallel", "parallel", "arbitrary"),
        name=f"dilated_attention_d{dilation}",
    )(slopes, qkv, qkv, qkv)
    return o.reshape(B * S, D), lse.reshape(B * S, D)


def _dil_merge_kernel(o0, o1, o2, l0, l1, l2, out_ref):
    a, b, c = l0[...], l1[...], l2[...]
    m = jnp.maximum(jnp.maximum(a, b), c)
    ea, eb, ec = jnp.exp(a - m), jnp.exp(b - m), jnp.exp(c - m)
    inv = 1.0 / (ea + eb + ec)
    out_ref[...] = ((ea * inv) * o0[...] + (eb * inv) * o1[...] + (ec * inv) * o2[...]).astype(out_ref.dtype)


def _dil_merge_res_ln_kernel(o0, o1, o2, l0, l1, l2, w_ref, x_ref, g_ref, b_ref, out_ref, *, alpha):
    def row_chunk(r0):
        rows = slice(r0, r0 + ROW_CHUNK)
        a, b, c = l0[rows, :], l1[rows, :], l2[rows, :]
        m = jnp.maximum(jnp.maximum(a, b), c)
        ea, eb, ec = jnp.exp(a - m), jnp.exp(b - m), jnp.exp(c - m)
        inv = 1.0 / (ea + eb + ec)
        h = ((ea * inv) * o0[rows, :] + (eb * inv) * o1[rows, :] + (ec * inv) * o2[rows, :]).astype(BF16)
        yield
        hw = _dot(h, w_ref[...])
        yield
        out_ref[rows, :] = _layer_norm(alpha * x_ref[rows, :] + hw, g_ref[...], b_ref[...])

    _run_pipelined([row_chunk(r0) for r0 in range(0, out_ref.shape[0], ROW_CHUNK)], skew=1)


def dilated_merge_residual_ln(outs, lses, w, x, g, b, alpha, tm):
    M, D = outs[0].shape
    row = pl.BlockSpec((tm, D), lambda i: (i, 0))
    fixed = lambda i: (0, 0)
    return pl.pallas_call(
        functools.partial(_dil_merge_res_ln_kernel, alpha=alpha),
        grid=(M // tm,),
        in_specs=[row] * 6 + [pl.BlockSpec((D, D), fixed), row,
                              pl.BlockSpec((1, D), fixed), pl.BlockSpec((1, D), fixed)],
        out_specs=row,
        out_shape=jax.ShapeDtypeStruct((M, D), F32),
        compiler_params=_params("parallel"),
        name="dilated_merge_residual_ln",
    )(*outs, *lses, w, x, g.reshape(1, D), b.reshape(1, D))


def dilated_merge(outs, lses, tm):
    M, D = outs[0].shape
    spec = pl.BlockSpec((tm, D), lambda i: (i, 0))
    return pl.pallas_call(
        _dil_merge_kernel,
        grid=(M // tm,),
        in_specs=[spec] * 6,
        out_specs=spec,
        out_shape=jax.ShapeDtypeStruct((M, D), BF16),
        compiler_params=_params("parallel"),
        name="dilated_merge",
    )(*outs, *lses)


def _shifted_rows(x, prev_ref, next_ref, i, n_i, reverse):
    tm = x.shape[0]
    r = lax.broadcasted_iota(jnp.int32, (tm, 1), 0)
    if reverse:
        edge = jnp.where(i == n_i - 1, 0.0, next_ref[0, 0:1, :])
        return jnp.where(r == tm - 1, edge, pltpu.roll(x, tm - 1, 0))
    edge = jnp.where(i == 0, 0.0, prev_ref[0, 7:8, :])
    return jnp.where(r == 0, edge, pltpu.roll(x, 1, 0))


def _rwkv_proj_kernel(x_ref, prev_ref, next_ref, mu_ref, wr_ref, wk_ref, wv_ref, w0_ref, w1_ref, w2_ref,
                      a0_ref, a1_ref, a2_ref, g1_ref, g2_ref,
                      r_ref, k_ref, v_ref, lw_ref, a_ref, g_ref, *, reverse):
    i = pl.program_id(1)
    x = x_ref[0]
    xx = _shifted_rows(x, prev_ref, next_ref, i, pl.num_programs(1), reverse) - x

    def projection(j, w_in, finish, out_ref):
        xm = (x + xx * mu_ref[j:j + 1, :]).astype(BF16)
        yield
        y = _dot(xm, w_in[...])
        yield
        out_ref[0] = finish(y).astype(out_ref.dtype)

    def decay(y):
        z = -(w0_ref[...] + _dot(jnp.tanh(y).astype(BF16), w2_ref[...]))
        softplus = jnp.maximum(z, 0.0) + jnp.log(1.0 + jnp.exp(-jnp.abs(z)))
        return -jnp.exp(-softplus - 0.5)

    def rate(y):
        return jax.nn.sigmoid(a0_ref[...] + _dot(y.astype(BF16), a2_ref[...]))

    def gate(y):
        return _dot(jax.nn.sigmoid(y).astype(BF16), g2_ref[...])

    same = lambda y: y
    _run_pipelined([projection(0, wr_ref, same, r_ref), projection(1, w1_ref, decay, lw_ref),
                    projection(2, wk_ref, same, k_ref), projection(4, a1_ref, rate, a_ref),
                    projection(3, wv_ref, same, v_ref), projection(5, g1_ref, gate, g_ref)], skew=1)


def rwkv_projections(x, mu, w_rkv, w0, w1, w2, a0, a1, a2, g1, g2, reverse, tm):
    B, S, D = x.shape
    nb = tm // 8
    fixed2 = lambda b, i: (0, 0)
    tile = pl.BlockSpec((1, tm, D), lambda b, i: (b, i, 0))
    full = lambda a: pl.BlockSpec(a.shape, fixed2)
    ops = (mu, w_rkv[0], w_rkv[1], w_rkv[2], w0.reshape(1, D), w1, w2, a0.reshape(1, D), a1, a2, g1, g2)
    return pl.pallas_call(
        functools.partial(_rwkv_proj_kernel, reverse=reverse),
        grid=(B, S // tm),
        in_specs=[tile,
                  pl.BlockSpec((1, 8, D), lambda b, i: (b, jnp.maximum(i * nb - 1, 0), 0)),
                  pl.BlockSpec((1, 8, D), lambda b, i: (b, jnp.minimum((i + 1) * nb, S // 8 - 1), 0))]
        + [full(a) for a in ops],
        out_specs=[tile] * 6,
        out_shape=[jax.ShapeDtypeStruct((B, S, D), dt) for dt in (BF16, BF16, BF16, F32, F32, BF16)],
        compiler_params=_params("parallel", "arbitrary"),
        name="rwkv_projections_bwd" if reverse else "rwkv_projections_fwd",
    )(x, x, x, *ops)


SLAB_HEADS = 4
SLAB = SLAB_HEADS * HEAD_DIM


def _split2(x):
    hi = x.astype(BF16)
    lo = (x - hi.astype(F32)).astype(BF16)
    return hi, lo


def _block_diag(x, block_mask):
    xb = x.astype(BF16)
    return jnp.where(block_mask, jnp.concatenate([xb] * SLAB_HEADS, 0), jnp.zeros((), BF16))


def _per_head_dot(lhs, x, block_mask, nt=False):
    mm = _dot_nt if nt else _dot
    return mm(lhs.astype(BF16), _block_diag(x, block_mask))


def _per_head_sums(xs, ones_bd):
    M = xs[0].shape[0]
    n = len(xs) * M
    s = _dot(jnp.concatenate(_split2(jnp.concatenate(xs, 0)), 0), ones_bd)
    tot = s[:n] + s[n:]
    return [tot[i * M:(i + 1) * M] for i in range(len(xs))]


def _wkv_slab_chunk(state, r, k, v, cum, lw, a, kkw, kaw, rkw, gng, gnb, gate, reverse, consts):
    block_mask, ones_bd, strict, incl, eye, lane_head = consts
    C = r.shape[0]
    kk = k * kkw
    k = k * (1.0 + (a - 1.0) * kaw)
    sums = yield jnp.concatenate([kk * kk, r * k * rkw], 0)
    kk = kk * lax.rsqrt(jnp.maximum(sums[:C], 1e-24))
    bonus = sums[C:]
    dec = jnp.exp(cum)
    inv = jnp.exp(-cum)
    alpha_bar = -kk * jnp.exp(cum - lw)
    beta_t = kk * a * inv
    k_t = k * inv
    r_bar = r * dec

    ar = jnp.concatenate([alpha_bar, r_bar], 0)
    gb = _per_head_dot(ar, beta_t, block_mask, nt=True)
    gk = _per_head_dot(ar, k_t, block_mask, nt=True)
    yield
    a_ab = jnp.where(strict, gb[:C], 0.0)
    a_rb = jnp.where(incl, gb[C:], 0.0)
    a_ak = jnp.where(strict, gk[:C], 0.0)
    a_rk = jnp.where(incl, gk[C:], 0.0)

    x = eye + a_ab
    p = _per_head_dot(a_ab, a_ab, block_mask)
    av = _per_head_dot(jnp.concatenate([a_ak, a_rk], 0), v, block_mask)
    ss = _per_head_dot(ar, state, block_mask, nt=True)
    yield
    m = 2
    while 2 * m < C:
        xp = _per_head_dot(jnp.concatenate([x, p], 0), p, block_mask)
        yield
        x = x + xp[:C]
        p = xp[C:]
        m *= 2
    xp = _per_head_dot(x, p, block_mask)
    yield
    x = x + xp
    u = _per_head_dot(x, ss[:C] + av[:C], block_mask)
    yield
    y = ss[C:] + _per_head_dot(a_rb, u, block_mask) + av[C:]

    total = cum[0:1] if reverse else cum[C - 1:C]
    end = jnp.exp(total)
    cross = lax.dot_general(jnp.concatenate([u, v], 0).astype(BF16),
                            jnp.concatenate([beta_t * end, k_t * end], 0).astype(BF16),
                            TN_DIMS, preferred_element_type=F32)
    mu = (yield y) * (1.0 / HEAD_DIM)
    new_state = state * end
    for hh in range(SLAB_HEADS):
        new_state = new_state + jnp.where(lane_head == hh, cross[hh * HEAD_DIM:(hh + 1) * HEAD_DIM], 0.0)

    yc = y - mu
    var = (yield yc * yc) * (1.0 / HEAD_DIM)
    out = yc * lax.rsqrt(var + GN_EPS) * gng + gnb + bonus * v
    return new_state, out * gate


def _wkv_kernel(rf, kf, vf, lwf, af, gf, rb, kb, vb, lwb, ab, gb,
                kk_ref, ka_ref, rk_ref, gng_ref, gnb_ref, of_ref, ob_ref, state_ref):
    c = pl.program_id(1)
    C = rf.shape[1]
    D = rf.shape[2]
    N = HEAD_DIM

    @pl.when(c == 0)
    def _():
        state_ref[...] = jnp.zeros_like(state_ref)

    row = lax.broadcasted_iota(jnp.int32, (SLAB, SLAB), 0)
    col = lax.broadcasted_iota(jnp.int32, (SLAB, SLAB), 1)
    block_mask = (row // N) == (col // N)
    ones_bd = jnp.where(block_mask, 1.0, 0.0).astype(BF16)
    t = lax.broadcasted_iota(jnp.int32, (C, SLAB), 0)
    lane = lax.broadcasted_iota(jnp.int32, (C, SLAB), 1)
    s = lane % C
    lane_head = lax.broadcasted_iota(jnp.int32, (N, SLAB), 1) // N
    ti = lax.broadcasted_iota(jnp.int32, (C, C), 0)
    si = lax.broadcasted_iota(jnp.int32, (C, C), 1)
    eye = (s == t).astype(F32)

    chains = []
    for d, (refs, o_ref) in enumerate((((rf, kf, vf, lwf, af, gf), of_ref), ((rb, kb, vb, lwb, ab, gb), ob_ref))):
        reverse = d == 1
        strict = (s > t) if reverse else (s < t)
        incl = (s >= t) if reverse else (s <= t)
        consts = (block_mask, ones_bd, strict, incl, eye, lane_head)
        tri = ((si >= ti) if reverse else (si <= ti)).astype(BF16)
        lw_all = refs[3][0]
        parts = _dot(tri, jnp.concatenate(_split2(lw_all), 1))
        cum_all = parts[:, :D] + parts[:, D:]
        for sb in range(D // SLAB):
            sl = slice(sb * SLAB, (sb + 1) * SLAB)
            r, k, v, lw, a, gate = (ref[0, :, sl].astype(F32) for ref in refs)
            chains.append((d, sb, sl, o_ref, _wkv_slab_chunk(
                state_ref[d, sb], r, k, v, cum_all[:, sl], lw, a,
                kk_ref[d:d + 1, sl], ka_ref[d:d + 1, sl], rk_ref[:, sl], gng_ref[:, sl], gnb_ref[:, sl],
                gate, reverse, consts)))

    slabs = [chain for *_, chain in chains]
    asks = [next(chain) for chain in slabs]
    results = [None] * len(slabs)
    while any(r is None for r in results):
        replies = [None] * len(slabs) if asks[0] is None else _per_head_sums(asks, ones_bd)
        for n, chain in enumerate(slabs):
            try:
                asks[n] = chain.send(replies[n])
            except StopIteration as stop:
                results[n] = stop.value
    for (d, sb, sl, o_ref, _), (new_state, out) in zip(chains, results):
        state_ref[d, sb] = new_state
        o_ref[0, :, sl] = out


def rwkv_scan(fwd, bwd, k_k, k_a, r_k, gn_g, gn_b):
    B, S, D = fwd[0].shape
    C = WKV_CHUNK
    NC = S // C
    f_spec = pl.BlockSpec((1, C, D), lambda b, c: (b, c, 0))
    b_spec = pl.BlockSpec((1, C, D), lambda b, c: (b, NC - 1 - c, 0))
    two = pl.BlockSpec((2, D), lambda b, c: (0, 0))
    one = pl.BlockSpec((1, D), lambda b, c: (0, 0))
    return pl.pallas_call(
        _wkv_kernel,
        grid=(B, NC),
        in_specs=[f_spec] * 6 + [b_spec] * 6 + [two, two, one, one, one],
        out_specs=[f_spec, b_spec],
        out_shape=[jax.ShapeDtypeStruct((B, S, D), F32)] * 2,
        scratch_shapes=[pltpu.VMEM((2, D // SLAB, HEAD_DIM, SLAB), F32)],
        compiler_params=_params("parallel", "arbitrary"),
        name="rwkv_scan",
    )(*fwd, *bwd, k_k, k_a, r_k.reshape(1, D), gn_g.reshape(1, D), gn_b.reshape(1, D))


def _xatt_kernel(x_ref, kv_ref, wq_ref, wo_ref, g_ref, b_ref, o_ref, *, alpha):
    D = x_ref.shape[-1]
    E = D // XATT_HEADS

    def row_chunk(r0):
        rows = slice(r0, r0 + ROW_CHUNK)
        q = (_dot(x_ref[0, rows, :].astype(BF16), wq_ref[...]) * (E ** -0.5)).astype(BF16)
        yield
        scores = [_dot_nt(q[:, h * E:(h + 1) * E], kv_ref[0, :, h * E:(h + 1) * E]) for h in range(XATT_HEADS)]
        yield
        probs, invs = [], []
        for s in scores:
            p = jnp.exp(s - jnp.max(s, -1, keepdims=True))
            invs.append(1.0 / jnp.sum(p, -1, keepdims=True))
            probs.append(p.astype(BF16))
        yield
        outs = [(_dot(probs[h], kv_ref[0, :, D + h * E:D + (h + 1) * E]) * invs[h]).astype(BF16)
                for h in range(XATT_HEADS)]
        yield
        y = _dot(jnp.concatenate(outs, -1), wo_ref[...])
        yield
        o_ref[0, rows, :] = _layer_norm(alpha * x_ref[0, rows, :] + y, g_ref[...], b_ref[...])

    _run_pipelined([row_chunk(r0) for r0 in range(0, o_ref.shape[1], ROW_CHUNK)], skew=1)


def cross_attention_sublayer(x, kv, w_q, w_o, g, b, alpha, tm):
    B, S, D = x.shape
    M = kv.shape[1]
    fixed = lambda bb, i: (0, 0)
    tile = pl.BlockSpec((1, tm, D), lambda bb, i: (bb, i, 0))
    return pl.pallas_call(
        functools.partial(_xatt_kernel, alpha=alpha),
        grid=(B, S // tm),
        in_specs=[tile, pl.BlockSpec((1, M, 2 * D), lambda bb, i: (bb, 0, 0)),
                  pl.BlockSpec((D, D), fixed), pl.BlockSpec((D, D), fixed),
                  pl.BlockSpec((1, D), fixed), pl.BlockSpec((1, D), fixed)],
        out_specs=tile,
        out_shape=jax.ShapeDtypeStruct((B, S, D), F32),
        compiler_params=_params("parallel", "arbitrary"),
        name="cross_attention",
    )(x, kv, w_q, w_o, g.reshape(1, D), b.reshape(1, D))


FFN_LANES = 128


def _ffn_in_kernel(x_ref, prev_ref, next_ref, wg_ref, wv_ref, cw_ref, cb_ref, o_ref, w_ref):
    i = pl.program_id(2)
    n_i = pl.num_programs(2)
    tm = x_ref.shape[1]
    C = FFN_LANES
    n_chunks = o_ref.shape[2] // C

    @pl.when((pl.program_id(1) == 0) & (i == 0))
    def _():
        for n in range(n_chunks):
            w_ref[:, 2 * C * n:2 * C * n + C] = wg_ref[:, C * n:C * (n + 1)]
            w_ref[:, 2 * C * n + C:2 * C * (n + 1)] = wv_ref[:, C * n:C * (n + 1)]

    x = jnp.concatenate([prev_ref[0], x_ref[0], next_ref[0]], 0).astype(BF16)
    r = lax.broadcasted_iota(jnp.int32, (tm, 1), 0)

    def column_chunk(n):
        y = _dot(x, w_ref[:, 2 * C * n:2 * C * (n + 1)])
        yield
        gate = y[8:8 + tm, :C]
        val = y[8:8 + tm, C:]
        g_prev = jnp.where(i == 0, 0.0, y[7:8, :C])
        g_next = jnp.where(i == n_i - 1, 0.0, y[tm + 8:tm + 9, :C])
        before = jnp.where(r == 0, g_prev, pltpu.roll(gate, 1, 0))
        after = jnp.where(r == tm - 1, g_next, pltpu.roll(gate, tm - 1, 0))
        cols = slice(C * n, C * (n + 1))
        conv = before * cw_ref[0:1, cols] + gate * cw_ref[1:2, cols] + after * cw_ref[2:3, cols] + cb_ref[:, cols]
        gelu = 0.5 * conv * (1.0 + lax.erf(conv * (2.0 ** -0.5)))
        o_ref[0, :, cols] = (gelu * val).astype(o_ref.dtype)

    _run_pipelined([column_chunk(n) for n in range(n_chunks)], skew=1)


def ffn_in(x, w_in, conv_w, conv_b, tm, tn):
    B, S, D = x.shape
    F = w_in.shape[1] // 2
    nb = tm // 8
    nj = F // tn
    return pl.pallas_call(
        _ffn_in_kernel,
        grid=(nj, B, S // tm),
        in_specs=[pl.BlockSpec((1, tm, D), lambda j, b, i: (b, i, 0)),
                  pl.BlockSpec((1, 8, D), lambda j, b, i: (b, jnp.maximum(i * nb - 1, 0), 0)),
                  pl.BlockSpec((1, 8, D), lambda j, b, i: (b, jnp.minimum((i + 1) * nb, S // 8 - 1), 0)),
                  pl.BlockSpec((D, tn), lambda j, b, i: (0, j)),
                  pl.BlockSpec((D, tn), lambda j, b, i: (0, nj + j)),
                  pl.BlockSpec((3, tn), lambda j, b, i: (0, j)),
                  pl.BlockSpec((1, tn), lambda j, b, i: (0, j))],
        out_specs=pl.BlockSpec((1, tm, tn), lambda j, b, i: (b, i, j)),
        out_shape=jax.ShapeDtypeStruct((B, S, F), BF16),
        scratch_shapes=[pltpu.VMEM((D, 2 * tn), BF16)],
        compiler_params=_params("arbitrary", "arbitrary", "arbitrary"),
        name="ffn_in",
    )(x, x, x, w_in, w_in, conv_w, conv_b.reshape(1, F))


def _alibi_slopes(n):
    return jnp.exp2(-8.0 * jnp.arange(1, n + 1, dtype=F32) / n)


def kernel(x, mem, diff_w_qkv, diff_lambda, diff_subln, diff_w_o, dil_w_qkv, dil_w_o, rwkv_mu, rwkv_w_rkv, rwkv_w0, rwkv_w1, rwkv_w2, rwkv_a0, rwkv_a1, rwkv_a2, rwkv_g1, rwkv_g2, rwkv_k_k, rwkv_k_a, rwkv_r_k, rwkv_gn_g, rwkv_gn_b, rwkv_w_o, xatt_w_q, xatt_w_kv, xatt_w_o, ffn_w_in, ffn_conv_w, ffn_conv_b, ffn_w_out, ln_g, ln_b):
    B, S, D = x.shape
    depth = xatt_w_q.shape[0]
    alpha = (2 * depth) ** 0.25
    T = B * S
    bf = lambda a: a.astype(BF16)
    mem2 = mem.reshape(-1, D)

    for i in range(depth):
        m, j = i % N_MIXERS, i // N_MIXERS
        x2 = x.reshape(T, D)
        if m == 0:
            lambda_init = 0.8 - 0.6 * math.exp(-0.3 * i)
            lamf = diff_lambda[j].astype(F32)
            lam_full = (jnp.exp(jnp.sum(lamf[0] * lamf[1])) - jnp.exp(jnp.sum(lamf[2] * lamf[3]))
                        + lambda_init).reshape(1)
            q_scale = jnp.where(jnp.arange(3 * D) < D, LOG2E * HEAD_DIM ** -0.5, 1.0).astype(F32)
            qkv = matmul(x2, bf(diff_w_qkv[j]), BF16, tm=1024, tn=3 * D, col_scale=q_scale).reshape(B, S, 3 * D)
            o = diff_attention_core(qkv, _split_slopes(LOG2E * _alibi_slopes(D // (2 * HEAD_DIM))), lam_full,
                                    diff_subln[j], lambda_init, tq=512)
            hs, w_o = [o.reshape(T, D)], diff_w_o[j]
        elif m == 1:
            qkv = matmul(x2, bf(dil_w_qkv[j]), F32, tm=512, tn=3 * D).reshape(B, S, -1)
            slopes = _alibi_slopes(D // HEAD_DIM)
            parts = [dilated_group_core(qkv, slopes, g, window, dilation, tq=128)
                     for g, (window, dilation) in enumerate(DIL_PATTERNS)]
            hs, w_o = None, dil_w_o[j]
        else:
            dirs = [rwkv_projections(x, rwkv_mu[j, d], bf(rwkv_w_rkv[j]), rwkv_w0[j, d], bf(rwkv_w1[j, d]),
                                     bf(rwkv_w2[j, d]), rwkv_a0[j, d], bf(rwkv_a1[j, d]), bf(rwkv_a2[j, d]),
                                     bf(rwkv_g1[j, d]), bf(rwkv_g2[j, d]), reverse=d == 1, tm=512)
                    for d in range(2)]
            yf, yb = rwkv_scan(dirs[0], dirs[1], rwkv_k_k[j], rwkv_k_a[j], rwkv_r_k[j],
                               rwkv_gn_g[j], rwkv_gn_b[j])
            hs, w_o = [yf.reshape(T, D), yb.reshape(T, D)], rwkv_w_o[j]
        if m == 1:
            x2 = dilated_merge_residual_ln([p[0] for p in parts], [p[1] for p in parts], bf(w_o), x2,
                                           ln_g[i, 0], ln_b[i, 0], alpha, tm=512)
        else:
            x2 = matmul_residual_ln(hs, bf(w_o), x2, ln_g[i, 0], ln_b[i, 0], alpha, tm=1024)

        kv = matmul(mem2, bf(xatt_w_kv[i]), BF16, tm=512, tn=2 * D).reshape(B, -1, 2 * D)
        x = cross_attention_sublayer(x2.reshape(B, S, D), kv, bf(xatt_w_q[i]), bf(xatt_w_o[i]),
                                     ln_g[i, 1], ln_b[i, 1], alpha, tm=1024)

        act = ffn_in(x, bf(ffn_w_in[i]), ffn_conv_w[i], ffn_conv_b[i], tm=512, tn=1408)
        x = matmul_residual_ln([act.reshape(T, -1)], bf(ffn_w_out[i]), x.reshape(T, D),
                               ln_g[i, 2], ln_b[i, 2], alpha, tm=1024).reshape(B, S, D)
    return x
```
